```python
import math
import jax
import jax.numpy as jnp
from jax import lax
import numpy as np

D_MODEL = 1024
BATCH = 8
SEQ = 4096
DEPTH = 2

CHUNK = 64
Q_BLOCK = 128
RMS_EPS = 1e-5
ROPE_THETA = 10000.0

POOL_WINDOWS = (2, 4, 8, 16)
POOL_GROUPS = len(POOL_WINDOWS)
POOL_WIDTH = D_MODEL // 2
POOL_GROUP_DIM = POOL_WIDTH // POOL_GROUPS

DIFF_HEADS = 4
DIFF_V_DIM = (D_MODEL // 2) // DIFF_HEADS
DIFF_QK_DIM = DIFF_V_DIM // 2
DIFF_QK_WIDTH = DIFF_HEADS * 2 * DIFF_QK_DIM
DIFF_WIDTH = DIFF_HEADS * DIFF_V_DIM
EVEN_IN_WIDTH = POOL_WIDTH + 2 * DIFF_QK_WIDTH + DIFF_WIDTH

RWKV_HEAD_DIM = 64
RWKV_HEADS = D_MODEL // RWKV_HEAD_DIM
DECAY_LORA = 64
AAA_LORA = 64
GATE_LORA = 128
RWKV_GN_EPS = 64e-5

D_FF = 2816
CONV_WIDTH = 3

kernel_name = 'hybrid_pool_diffattn_rwkv7_convffn'


def rms_norm(x, g):
    xf = x.astype(jnp.float32)
    y = xf * lax.rsqrt(jnp.mean(xf * xf, axis=-1, keepdims=True) + RMS_EPS)
    return (y * g.astype(jnp.float32)).astype(x.dtype)


def rotary_tables(seq, dim):
    inv_freq = 1.0 / (ROPE_THETA ** (jnp.arange(0, dim, 2, dtype=jnp.float32) / dim))
    ang = jnp.arange(seq, dtype=jnp.float32)[:, None] * inv_freq[None, :]
    return jnp.cos(ang), jnp.sin(ang)


def apply_rotary(x, cos, sin):
    half = x.shape[-1] // 2
    xf = x.astype(jnp.float32)
    x1, x2 = xf[..., :half], xf[..., half:]
    c = cos[None, :, None, None, :]
    s = sin[None, :, None, None, :]
    return jnp.concatenate([x1 * c - x2 * s, x2 * c + x1 * s], axis=-1).astype(x.dtype)


def pool_mixer(u, pool_w, pool_scale):
    b, s, _ = u.shape
    uf = u.astype(jnp.float32).reshape(b, s, POOL_GROUPS, POOL_GROUP_DIM)
    csum = jnp.cumsum(uf, axis=1)
    pos = jnp.arange(s)
    pooled = []
    for gi, win in enumerate(POOL_WINDOWS):
        c = csum[:, :, gi]
        c_prev = jnp.pad(c, ((0, 0), (win, 0), (0, 0)))[:, :s]
        count = jnp.minimum(pos + 1, win).astype(jnp.float32)[None, :, None]
        pooled.append((c - c_prev) / count)
    delta = jnp.stack(pooled, axis=2) - uf
    y = jnp.einsum('bsgc,gcd->bsgd', delta, pool_w.astype(jnp.float32))
    y = y.reshape(b, s, POOL_WIDTH) * pool_scale.astype(jnp.float32)
    return y.astype(u.dtype)


def diff_attention(q, k, v, lam, subln_g, lambda_init):
    b, h, _, s, dqk = q.shape
    scale = dqk ** -0.5
    neg = jnp.finfo(jnp.float32).min
    outs = []
    for blk in range(s // Q_BLOCK):
        q0 = blk * Q_BLOCK
        kend = q0 + Q_BLOCK
        scores = jnp.einsum('bhmqd,bhmkd->bhmqk', q[:, :, :, q0:kend],
                            k[:, :, :, :kend]).astype(jnp.float32) * scale
        q_chunk = (q0 + jnp.arange(Q_BLOCK)) // CHUNK
        k_chunk = jnp.arange(kend) // CHUNK
        mask = k_chunk[None, :] <= q_chunk[:, None]
        probs = jax.nn.softmax(jnp.where(mask, scores, neg), axis=-1)
        weights = probs[:, :, 0] - lam * probs[:, :, 1]
        outs.append(jnp.einsum('bhqk,bhkd->bhqd', weights.astype(v.dtype), v[:, :, :kend]))
    o = jnp.concatenate(outs, axis=2)
    o = rms_norm(o, subln_g).astype(jnp.float32) * (1.0 - lambda_init)
    return o.transpose(0, 2, 1, 3).reshape(b, s, h * v.shape[-1]).astype(v.dtype)


def even_mixer(h, w_in, pool_w, pool_scale, lam_vecs, subln_g, w_o, lambda_init):
    b, s, _ = h.shape
    proj = h @ w_in
    u = proj[..., :POOL_WIDTH]
    o1 = POOL_WIDTH
    o2 = o1 + DIFF_QK_WIDTH
    o3 = o2 + DIFF_QK_WIDTH
    q = proj[..., o1:o2].reshape(b, s, DIFF_HEADS, 2, DIFF_QK_DIM)
    k = proj[..., o2:o3].reshape(b, s, DIFF_HEADS, 2, DIFF_QK_DIM)
    v = proj[..., o3:].reshape(b, s, DIFF_HEADS, DIFF_V_DIM).transpose(0, 2, 1, 3)
    cos, sin = rotary_tables(s, DIFF_QK_DIM)
    q = apply_rotary(q, cos, sin).transpose(0, 2, 3, 1, 4)
    k = apply_rotary(k, cos, sin).transpose(0, 2, 3, 1, 4)
    lv = lam_vecs.astype(jnp.float32)
    lam = jnp.exp(jnp.sum(lv[0] * lv[1])) - jnp.exp(jnp.sum(lv[2] * lv[3])) + lambda_init
    pool_out = pool_mixer(u, pool_w, pool_scale)
    attn_out = diff_attention(q, k, v, lam, subln_g, lambda_init)
    return jnp.concatenate([pool_out, attn_out], axis=-1) @ w_o


def rwkv_step(state, inp):
    r_t, w_t, k_t, v_t, a_t, b_t = inp
    sa = jnp.einsum('bhij,bhj->bhi', state, a_t)
    state = (state * w_t[:, :, None, :] + sa[..., None] * b_t[:, :, None, :]
             + v_t[..., None] * k_t[:, :, None, :])
    return state, jnp.einsum('bhij,bhj->bhi', state, r_t)


def rwkv7_mixer(h, mu, w_r, w_k, w_v, w_o, w0, w1, w2, a0, a1, a2, g1, g2,
                k_k, k_a, r_k, lnx_w, lnx_b):
    b, s, d = h.shape
    nh, hd = RWKV_HEADS, RWKV_HEAD_DIM
    f32 = jnp.float32
    xx = jnp.pad(h, ((0, 0), (1, 0), (0, 0)))[:, :s] - h
    xr, xw, xk, xv, xa, xg = [h + xx * mu[i] for i in range(6)]
    r = (xr @ w_r).astype(f32)
    k = (xk @ w_k).astype(f32)
    v = (xv @ w_v).astype(f32)
    w_log = -jax.nn.softplus(-(w0 + jnp.tanh(xw @ w1) @ w2).astype(f32)) - 0.5
    decay = jnp.exp(-jnp.exp(w_log))
    a = jax.nn.sigmoid((a0 + (xa @ a1) @ a2).astype(f32))
    g = (jax.nn.sigmoid(xg @ g1) @ g2).astype(f32)
    kk = (k * k_k.astype(f32)).reshape(b, s, nh, hd)
    kk = kk / jnp.maximum(jnp.sqrt(jnp.sum(kk * kk, axis=-1, keepdims=True)), 1e-12)
    k = k * (1.0 + (a - 1.0) * k_a.astype(f32))

    def heads_tm(t):
        return t.reshape(b, s, nh, hd).transpose(1, 0, 2, 3)

    a_h = a.reshape(b, s, nh, hd)
    seqs = (heads_tm(r), heads_tm(decay), heads_tm(k), heads_tm(v),
            (-kk).transpose(1, 0, 2, 3), (kk * a_h).transpose(1, 0, 2, 3))
    state0 = jnp.zeros((b, nh, hd, hd), f32)
    _, y = lax.scan(rwkv_step, state0, seqs)
    y = y.transpose(1, 0, 2, 3)
    mean = jnp.mean(y, axis=-1, keepdims=True)
    var = jnp.mean(jnp.square(y - mean), axis=-1, keepdims=True)
    y = ((y - mean) * lax.rsqrt(var + RWKV_GN_EPS)).reshape(b, s, d)
    y = y * lnx_w.astype(f32) + lnx_b.astype(f32)
    bonus = jnp.sum(r.reshape(b, s, nh, hd) * k.reshape(b, s, nh, hd) * r_k.astype(f32),
                    axis=-1, keepdims=True) * v.reshape(b, s, nh, hd)
    y = y + bonus.reshape(b, s, d)
    return (y * g).astype(h.dtype) @ w_o


def conv_ffn(h, w_in, conv_w, conv_b, w_out):
    ag = h @ w_in
    a, gate = ag[..., :D_FF], ag[..., D_FF:]
    a = lax.conv_general_dilated(a, conv_w[:, None, :], window_strides=(1,),
                                 padding=[(CONV_WIDTH - 1, 0)],
                                 dimension_numbers=('NWC', 'WIO', 'NWC'),
                                 feature_group_count=D_FF) + conv_b
    return (jax.nn.gelu(a, approximate=False) * gate) @ w_out


def setup_inputs(seed: int = 0) -> dict:
    key = jax.random.key(seed)
    ks = iter(jax.random.split(key, 48))
    f32 = jnp.float32
    d = D_MODEL
    ne = (DEPTH + 1) // 2
    no = DEPTH // 2

    def nrm(shape, scale):
        return jax.random.normal(next(ks), shape, f32) * scale

    x = nrm((BATCH, SEQ, d), 1.0)
    e_norm_g = 1.0 + nrm((ne, d), 0.02)
    e_w_in = nrm((ne, d, EVEN_IN_WIDTH), d ** -0.5)
    e_pool_w = nrm((ne, POOL_GROUPS, POOL_GROUP_DIM, POOL_GROUP_DIM), POOL_GROUP_DIM ** -0.5)
    e_pool_scale = 1.0 + nrm((ne, POOL_WIDTH), 0.02)
    e_lambda = nrm((ne, 4, DIFF_QK_DIM), 0.1)
    e_subln_g = 1.0 + nrm((ne, DIFF_V_DIM), 0.02)
    e_w_o = nrm((ne, d, d), d ** -0.5)

    o_norm_g = 1.0 + nrm((no, d), 0.02)
    o_mu = jax.random.uniform(next(ks), (no, 6, d), f32)
    o_w_r = nrm((no, d, d), d ** -0.5)
    o_w_k = nrm((no, d, d), d ** -0.5)
    o_w_v = nrm((no, d, d), d ** -0.5)
    o_w_o = nrm((no, d, d), d ** -0.5)
    ramp = jnp.linspace(0.0, 1.0, d, dtype=f32)
    o_w0 = (-6.5 + 5.0 * ramp ** 0.85)[None, :] + nrm((no, d), 0.1)
    o_w1 = nrm((no, d, DECAY_LORA), d ** -0.5)
    o_w2 = nrm((no, DECAY_LORA, d), 0.1 * DECAY_LORA ** -0.5)
    o_a0 = nrm((no, d), 0.1)
    o_a1 = nrm((no, d, AAA_LORA), d ** -0.5)
    o_a2 = nrm((no, AAA_LORA, d), 0.1 * AAA_LORA ** -0.5)
    o_g1 = nrm((no, d, GATE_LORA), d ** -0.5)
    o_g2 = nrm((no, GATE_LORA, d), GATE_LORA ** -0.5)
    o_k_k = 0.85 + nrm((no, d), 0.02)
    o_k_a = 1.0 + nrm((no, d), 0.02)
    o_r_k = -0.04 + nrm((no, RWKV_HEADS, RWKV_HEAD_DIM), 0.1)
    o_lnx_w = 1.0 + nrm((no, d), 0.02)
    o_lnx_b = nrm((no, d), 0.01)

    f_norm_g = 1.0 + nrm((DEPTH, d), 0.02)
    f_w_in = nrm((DEPTH, d, 2 * D_FF), d ** -0.5)
    f_conv_w = nrm((DEPTH, CONV_WIDTH, D_FF), CONV_WIDTH ** -0.5)
    f_conv_b = nrm((DEPTH, D_FF), 0.01)
    f_w_out = nrm((DEPTH, D_FF, d), D_FF ** -0.5)
    final_g = 1.0 + nrm((d,), 0.02)
    return {'x': x, 'e_norm_g': e_norm_g, 'e_w_in': e_w_in, 'e_pool_w': e_pool_w,
            'e_pool_scale': e_pool_scale, 'e_lambda': e_lambda, 'e_subln_g': e_subln_g,
            'e_w_o': e_w_o, 'o_norm_g': o_norm_g, 'o_mu': o_mu, 'o_w_r': o_w_r,
            'o_w_k': o_w_k, 'o_w_v': o_w_v, 'o_w_o': o_w_o, 'o_w0': o_w0, 'o_w1': o_w1,
            'o_w2': o_w2, 'o_a0': o_a0, 'o_a1': o_a1, 'o_a2': o_a2, 'o_g1': o_g1,
            'o_g2': o_g2, 'o_k_k': o_k_k, 'o_k_a': o_k_a, 'o_r_k': o_r_k,
            'o_lnx_w': o_lnx_w, 'o_lnx_b': o_lnx_b, 'f_norm_g': f_norm_g,
            'f_w_in': f_w_in, 'f_conv_w': f_conv_w, 'f_conv_b': f_conv_b,
            'f_w_out': f_w_out, 'final_g': final_g}


def reference(x, e_norm_g, e_w_in, e_pool_w, e_pool_scale, e_lambda, e_subln_g, e_w_o,
              o_norm_g, o_mu, o_w_r, o_w_k, o_w_v, o_w_o, o_w0, o_w1, o_w2, o_a0, o_a1,
              o_a2, o_g1, o_g2, o_k_k, o_k_a, o_r_k, o_lnx_w, o_lnx_b, f_norm_g, f_w_in,
              f_conv_w, f_conv_b, f_w_out, final_g):
    for layer in range(DEPTH):
        i = layer // 2
        if layer % 2 == 0:
            lambda_init = 0.8 - 0.6 * math.exp(-0.3 * layer)
            mix = even_mixer(rms_norm(x, e_norm_g[i]), e_w_in[i], e_pool_w[i],
                             e_pool_scale[i], e_lambda[i], e_subln_g[i], e_w_o[i],
                             lambda_init)
        else:
            mix = rwkv7_mixer(rms_norm(x, o_norm_g[i]), o_mu[i], o_w_r[i], o_w_k[i],
                              o_w_v[i], o_w_o[i], o_w0[i], o_w1[i], o_w2[i], o_a0[i],
                              o_a1[i], o_a2[i], o_g1[i], o_g2[i], o_k_k[i], o_k_a[i],
                              o_r_k[i], o_lnx_w[i], o_lnx_b[i])
        x = x + mix.astype(x.dtype)
        ffn = conv_ffn(rms_norm(x, f_norm_g[layer]), f_w_in[layer], f_conv_w[layer],
                       f_conv_b[layer], f_w_out[layer])
        x = x + ffn.astype(x.dtype)
    return rms_norm(x, final_g)
```

```python
import functools
import math

import jax
import jax.numpy as jnp
from jax import lax
from jax.experimental import pallas as pl
from jax.experimental.pallas import tpu as pltpu

F32 = jnp.float32
BF16 = jnp.bfloat16

RMS_EPS = 1e-5
ROPE_THETA = 10000.0
CHUNK = 64
POOL_WINDOWS = (2, 4, 8, 16)
POOL_HALO = 16
GROUP = 128
HEAD = 64
RWKV_GN_EPS = 64e-5
RWKV_L = 64
LORA_PAD = 128
HALO = 8
NEG = -1e30
VMEM_LIMIT = 56 * 1024 * 1024


def _rms(x, g):
    return x * lax.rsqrt(jnp.mean(x * x, axis=-1, keepdims=True) + RMS_EPS) * g


def _dot(a, b):
    return jnp.dot(a, b, preferred_element_type=F32)


def _dot_nt(a, b):
    return lax.dot_general(a, b, (((1,), (1,)), ((), ())), preferred_element_type=F32)


def _dot_tn(a, b):
    return lax.dot_general(a, b, (((0,), (0,)), ((), ())), preferred_element_type=F32)


def _const_spec(shape):
    return pl.BlockSpec(shape, lambda *_: (0,) * len(shape))


def _params(sem):
    return pltpu.CompilerParams(dimension_semantics=sem, vmem_limit_bytes=VMEM_LIMIT)


def _even_proj_kernel(x_ref, g_ref, w_ref, cos_ref, sin_ref, u_ref, q_ref, k_ref, v_ref):
    width = u_ref.shape[1]
    xn = _rms(x_ref[...], g_ref[...]).astype(BF16)
    cos = jnp.concatenate([cos_ref[...]] * (width // GROUP), axis=1)
    sin = jnp.concatenate([sin_ref[...]] * (width // GROUP), axis=1)
    lane = lax.broadcasted_iota(jnp.int32, cos.shape, 1)
    first_half = (lane % HEAD) < HEAD // 2

    def proj(idx):
        return _dot(xn, w_ref[:, idx * width:(idx + 1) * width])

    def rope(t):
        partner = jnp.where(first_half, pltpu.roll(t, width - HEAD // 2, 1),
                            pltpu.roll(t, HEAD // 2, 1))
        return t * cos + partner * sin

    u_ref[...] = proj(0)
    q_ref[...] = (rope(proj(1)) * (HEAD ** -0.5)).astype(BF16)
    k_ref[...] = rope(proj(2)).astype(BF16)
    v_ref[...] = proj(3).astype(BF16)


def _even_proj(x2, g, w, cos, sin, seq, tm):
    rows, d = x2.shape
    width = w.shape[1] // 4
    tiles_per_seq = seq // tm
    row_spec = pl.BlockSpec((tm, d), lambda i: (i, 0))
    out_spec = pl.BlockSpec((tm, width), lambda i: (i, 0))
    tab_spec = pl.BlockSpec((tm, GROUP), lambda i: (i % tiles_per_seq, 0))
    return pl.pallas_call(
        _even_proj_kernel,
        grid=(rows // tm,),
        in_specs=[row_spec, _const_spec((1, d)), _const_spec(w.shape), tab_spec, tab_spec],
        out_specs=[out_spec] * 4,
        out_shape=[jax.ShapeDtypeStruct((rows, width), F32)]
        + [jax.ShapeDtypeStruct((rows, width), BF16)] * 3,
        compiler_params=_params(("parallel",)),
        name="even_proj",
    )(x2, g, w, cos, sin)


def _pool_kernel(u_ref, pw_ref, ps_ref, o_ref, buf):
    tp = u_ref.shape[0]
    t = pl.program_id(1)

    @pl.when(t == 0)
    def _():
        buf[0:POOL_HALO, :] = jnp.zeros((POOL_HALO, buf.shape[1]), F32)

    u = u_ref[...]
    buf[POOL_HALO:, :] = u
    pos = t * tp + lax.broadcasted_iota(jnp.int32, (tp, 1), 0)
    for gi, win in enumerate(POOL_WINDOWS):
        sl = slice(gi * GROUP, (gi + 1) * GROUP)
        ug = u[:, sl]
        acc = ug
        for j in range(1, win):
            acc = acc + buf[POOL_HALO - j:POOL_HALO - j + tp, sl]
        count = jnp.minimum(pos + 1, win).astype(F32)
        delta = acc / count - ug
        y = _dot(delta.astype(BF16), pw_ref[gi]) * ps_ref[:, sl]
        o_ref[:, sl] = y.astype(BF16)
    buf[0:POOL_HALO, :] = buf[tp:tp + POOL_HALO, :]


def _pool(u, pool_w, pool_scale, batch, seq, tp):
    rows, width = u.shape
    nt = seq // tp
    spec = pl.BlockSpec((tp, width), lambda b, t: (b * nt + t, 0))
    return pl.pallas_call(
        _pool_kernel,
        grid=(batch, nt),
        in_specs=[spec, _const_spec(pool_w.shape), _const_spec((1, width))],
        out_specs=spec,
        out_shape=jax.ShapeDtypeStruct((rows, width), BF16),
        scratch_shapes=[pltpu.VMEM((POOL_HALO + tp, width), F32)],
        compiler_params=_params(("arbitrary", "arbitrary")),
        name="pool_mixer",
    )(u, pool_w, pool_scale)


def _attn_kernel(lam_ref, q_ref, k_ref, v_ref, sg_ref, o_ref, *, lambda_init):
    t = q_ref.shape[0]
    i = pl.program_id(2)
    q = q_ref[...]
    lane = lax.broadcasted_iota(jnp.int32, q.shape, 1)
    zero = jnp.zeros_like(q)
    qs = jnp.concatenate([jnp.where(lane < HEAD, q, zero), jnp.where(lane < HEAD, zero, q)], axis=0)

    def block(j, carry, diagonal):
        m, l, acc = carry
        start = pl.multiple_of(j * t, t)
        kb = k_ref[pl.ds(start, t), :]
        vb = v_ref[pl.ds(start, t), :]
        s = _dot_nt(qs, kb)
        if diagonal:
            row = lax.broadcasted_iota(jnp.int32, s.shape, 0)
            col = lax.broadcasted_iota(jnp.int32, s.shape, 1)
            s = jnp.where((col // CHUNK) <= ((row % t) // CHUNK), s, NEG)
        m_new = jnp.maximum(m, jnp.max(s, axis=-1, keepdims=True))
        alpha = jnp.exp(m - m_new)
        p = jnp.exp(s - m_new)
        l = alpha * l + jnp.sum(p, axis=-1, keepdims=True)
        acc = alpha * acc + _dot(p.astype(BF16), vb)
        return m_new, l, acc

    init = (jnp.full((2 * t, 1), NEG, F32), jnp.zeros((2 * t, 1), F32),
            jnp.zeros((2 * t, GROUP), F32))
    carry = lax.fori_loop(0, i, lambda j, c: block(j, c, False), init)
    _, l, acc = block(i, carry, True)
    o = acc / l
    lv = lam_ref[...]
    lam = (jnp.exp(jnp.sum(lv[0:1] * lv[1:2], axis=-1, keepdims=True))
           - jnp.exp(jnp.sum(lv[2:3] * lv[3:4], axis=-1, keepdims=True)) + lambda_init)
    o = o[:t] - lam * o[t:]
    o = _rms(o, sg_ref[...]) * (1.0 - lambda_init)
    o_ref[...] = o.astype(BF16)


def _diff_attention(q, k, v, lam_vecs, subln_g, batch, seq, lambda_init, t):
    rows, width = q.shape
    heads = width // GROUP
    nq = seq // t
    q_spec = pl.BlockSpec((t, GROUP), lambda b, h, i: (b * nq + i, h))
    kv_spec = pl.BlockSpec((seq, GROUP), lambda b, h, i: (b, h))
    return pl.pallas_call(
        functools.partial(_attn_kernel, lambda_init=lambda_init),
        grid=(batch, heads, nq),
        in_specs=[_const_spec(lam_vecs.shape), q_spec, kv_spec, kv_spec, _const_spec((1, GROUP))],
        out_specs=q_spec,
        out_shape=jax.ShapeDtypeStruct((rows, width), BF16),
        compiler_params=_params(("parallel", "parallel", "arbitrary")),
        name="diff_attention",
    )(lam_vecs, q, k, v, subln_g)


def _proj_res_kernel(*refs, n_in):
    x_ref, o_ref = refs[0], refs[-1]
    acc = x_ref[...]
    for a_ref, w_ref in zip(refs[1:1 + n_in], refs[1 + n_in:1 + 2 * n_in]):
        acc = acc + _dot(a_ref[...], w_ref[...])
    o_ref[...] = acc


def _proj_res(x2, acts, weights, tm):
    rows, d = x2.shape
    in_specs = [pl.BlockSpec((tm, d), lambda i: (i, 0))]
    in_specs += [pl.BlockSpec((tm, a.shape[1]), lambda i: (i, 0)) for a in acts]
    in_specs += [_const_spec(w.shape) for w in weights]
    return pl.pallas_call(
        functools.partial(_proj_res_kernel, n_in=len(acts)),
        grid=(rows // tm,),
        in_specs=in_specs,
        out_specs=pl.BlockSpec((tm, d), lambda i: (i, 0)),
        out_shape=jax.ShapeDtypeStruct((rows, d), F32),
        compiler_params=_params(("parallel",)),
        name="proj_residual",
    )(x2, *acts, *weights)


def _ffn_kernel(x_ref, halo_ref, g_ref, wa_ref, wg_ref, cw_ref, cb_ref, wo_ref, fg_ref, o_ref,
                abuf, hbuf, *, tiles_per_seq, tf, final_norm):
    tm = x_ref.shape[0]
    d_ff = hbuf.shape[1]
    x = x_ref[...]
    g = g_ref[...]
    xn = _rms(x, g).astype(BF16)
    seq_start = (pl.program_id(0) % tiles_per_seq) == 0
    xh = jnp.where(seq_start, 0.0, _rms(halo_ref[...], g)).astype(BF16)

    def chunk(c, carry):
        col = pl.multiple_of(c * tf, tf)
        wa = wa_ref[:, pl.ds(col, tf)]
        abuf[0:HALO, :] = _dot(xh, wa)
        a = _dot(xn, wa)
        abuf[HALO:, :] = a
        gate = _dot(xn, wg_ref[:, pl.ds(col, tf)])
        cw = cw_ref[:, pl.ds(col, tf)]
        conv = (abuf[HALO - 2:HALO - 2 + tm, :] * cw[0:1] + abuf[HALO - 1:HALO - 1 + tm, :] * cw[1:2]
                + a * cw[2:3] + cb_ref[:, pl.ds(col, tf)])
        act = 0.5 * conv * (1.0 + lax.erf(conv * (2.0 ** -0.5)))
        hbuf[:, pl.ds(col, tf)] = (act * gate).astype(BF16)
        return carry

    lax.fori_loop(0, d_ff // tf, chunk, 0)
    out = x + _dot(hbuf[...], wo_ref[...])
    if final_norm:
        out = _rms(out, fg_ref[...])
    o_ref[...] = out


def _ffn(x2, g, wa, wg, conv_w, conv_b, w_out, final_g, seq, tm, tf, final_norm):
    rows, d = x2.shape
    d_ff = wa.shape[1]
    tiles_per_seq = seq // tm
    halo_blocks = tm // HALO
    row_spec = pl.BlockSpec((tm, d), lambda i: (i, 0))
    halo_spec = pl.BlockSpec((HALO, d), lambda i: (jnp.maximum(i * halo_blocks - 1, 0), 0))
    return pl.pallas_call(
        functools.partial(_ffn_kernel, tiles_per_seq=tiles_per_seq, tf=tf, final_norm=final_norm),
        grid=(rows // tm,),
        in_specs=[row_spec, halo_spec, _const_spec((1, d)), _const_spec(wa.shape),
                  _const_spec(wg.shape), _const_spec(conv_w.shape), _const_spec((1, d_ff)),
                  _const_spec(w_out.shape), _const_spec((1, d))],
        out_specs=row_spec,
        out_shape=jax.ShapeDtypeStruct((rows, d), F32),
        scratch_shapes=[pltpu.VMEM((HALO + tm, tf), F32), pltpu.VMEM((tm, d_ff), BF16)],
        compiler_params=_params(("parallel",)),
        name="conv_ffn",
    )(x2, x2, g, wa, wg, conv_w, conv_b, w_out, final_g)


def _rwkv_proj_kernel(x_ref, halo_ref, g_ref, mu_ref, wr_ref, wk_ref, wv_ref, w0_ref, w1_ref,
                      w2_ref, a0_ref, a1_ref, a2_ref, g1_ref, g2_ref, kk_ref, ka_ref,
                      r_out, ld_out, k_out, v_out, kk_out, a_out, g_out, hbuf, *, tiles_per_seq):
    tm = x_ref.shape[0]
    g = g_ref[...]
    h = _rms(x_ref[...], g)
    seq_start = (pl.program_id(0) % tiles_per_seq) == 0
    hbuf[0:HALO, :] = jnp.where(seq_start, 0.0, _rms(halo_ref[...], g))
    hbuf[HALO:, :] = h
    xx = hbuf[HALO - 1:HALO - 1 + tm, :] - h

    def mix(idx):
        return (h + xx * mu_ref[idx:idx + 1, :]).astype(BF16)

    r_out[...] = _dot(mix(0), wr_ref[...])
    w_lin = w0_ref[...] + _dot(jnp.tanh(_dot(mix(1), w1_ref[...])).astype(BF16), w2_ref[...])
    w_log = -jax.nn.softplus(-w_lin) - 0.5
    ld_out[...] = -jnp.exp(w_log)
    k = _dot(mix(2), wk_ref[...])
    v_out[...] = _dot(mix(3), wv_ref[...])
    a = jax.nn.sigmoid(a0_ref[...] + _dot(_dot(mix(4), a1_ref[...]).astype(BF16), a2_ref[...]))
    a_out[...] = a
    g_out[...] = _dot(jax.nn.sigmoid(_dot(mix(5), g1_ref[...])).astype(BF16), g2_ref[...])
    kk_out[...] = k * kk_ref[...]
    k_out[...] = k * (1.0 + (a - 1.0) * ka_ref[...])


def _rwkv_proj(x2, g, mu, wr, wk, wv, w0, w1, w2, a0, a1, a2, g1, g2, k_k, k_a, seq, tm):
    rows, d = x2.shape
    tiles_per_seq = seq // tm
    halo_blocks = tm // HALO
    row_spec = pl.BlockSpec((tm, d), lambda i: (i, 0))
    halo_spec = pl.BlockSpec((HALO, d), lambda i: (jnp.maximum(i * halo_blocks - 1, 0), 0))
    vec = _const_spec((1, d))
    consts = [mu, wr, wk, wv, w0, w1, w2, a0, a1, a2, g1, g2, k_k, k_a]
    return pl.pallas_call(
        functools.partial(_rwkv_proj_kernel, tiles_per_seq=tiles_per_seq),
        grid=(rows // tm,),
        in_specs=[row_spec, halo_spec, vec] + [_const_spec(c.shape) for c in consts],
        out_specs=[row_spec] * 7,
        out_shape=[jax.ShapeDtypeStruct((rows, d), F32)] * 7,
        scratch_shapes=[pltpu.VMEM((HALO + tm, d), F32)],
        compiler_params=_params(("parallel",)),
        name="rwkv_proj",
    )(x2, x2, g, *consts)


def _rwkv_rec_kernel(r_ref, ld_ref, k_ref, v_ref, kk_ref, a_ref, g_ref, rk_ref, lw_ref, lb_ref,
                     o_ref, state):
    L = r_ref.shape[0]
    pairs = r_ref.shape[1] // GROUP

    @pl.when(pl.program_id(2) == 0)
    def _():
        state[...] = jnp.zeros(state.shape, F32)

    lane = lax.broadcasted_iota(jnp.int32, (L, GROUP), 1)
    head0 = lane < HEAD

    def split(t):
        return jnp.concatenate([jnp.where(head0, t, 0.0), jnp.where(head0, 0.0, t)], axis=0)

    def head_sum(t):
        s0 = jnp.sum(jnp.where(head0, t, 0.0), axis=-1, keepdims=True)
        s1 = jnp.sum(jnp.where(head0, 0.0, t), axis=-1, keepdims=True)
        return jnp.where(head0, s0, s1)

    row = lax.broadcasted_iota(jnp.int32, (4 * L, 2 * L), 0)
    col = lax.broadcasted_iota(jnp.int32, (4 * L, 2 * L), 1)
    same_head = ((row // L) % 2) == (col // L)
    strict = (col % L) < (row % L)
    keep = same_head & (strict | ((row >= 2 * L) & ((col % L) == (row % L))))
    r2 = lax.broadcasted_iota(jnp.int32, (2 * L, 2 * L), 0)
    c2 = lax.broadcasted_iota(jnp.int32, (2 * L, 2 * L), 1)
    eye = (r2 == c2).astype(F32)
    block_diag = (r2 // HEAD) == (c2 // HEAD)
    tl = lax.broadcasted_iota(jnp.int32, (L, L), 0)
    tc = lax.broadcasted_iota(jnp.int32, (L, L), 1)
    tri = (tc <= tl).astype(BF16)

    for p in range(pairs):
        sl = slice(p * GROUP, (p + 1) * GROUP)
        r, ld, kx, v = r_ref[:, sl], ld_ref[:, sl], k_ref[:, sl], v_ref[:, sl]
        kk, a = kk_ref[:, sl], a_ref[:, sl]
        kkn = kk / jnp.maximum(jnp.sqrt(head_sum(kk * kk)), 1e-12)
        ld_hi = ld.astype(BF16)
        rem = ld - ld_hi.astype(F32)
        ld_mid = rem.astype(BF16)
        ld_lo = (rem - ld_mid.astype(F32)).astype(BF16)
        c3 = _dot(tri, jnp.concatenate([ld_hi, ld_mid, ld_lo], axis=1))
        cum = c3[:, :GROUP] + c3[:, GROUP:2 * GROUP] + c3[:, 2 * GROUP:]
        p_in = jnp.exp(cum)
        inv_p = jnp.exp(-cum)
        rt = r * p_in
        at = -kkn * jnp.exp(cum - ld)
        bt = kkn * a * inv_p
        kt = kx * inv_p

        lhs = jnp.concatenate([split(at), split(rt)], axis=0).astype(BF16)
        xb = jnp.where(keep, _dot_nt(lhs, jnp.concatenate([bt, bt], axis=0).astype(BF16)), 0.0)
        xk = jnp.where(keep, _dot_nt(lhs, jnp.concatenate([kt, kt], axis=0).astype(BF16)), 0.0)
        m_ab, m_rb = xb[:2 * L], xb[2 * L:]
        m_ak, m_rk = xk[:2 * L], xk[2 * L:]

        n = m_ab.astype(BF16)
        t_inv = eye + m_ab
        for _ in range(int(math.log2(L)) - 1):
            n32 = _dot(n, n)
            n = n32.astype(BF16)
            t_inv = t_inv + _dot(t_inv.astype(BF16), n)

        s_bd = state[p]
        xs = _dot_nt(lhs, s_bd.astype(BF16))
        v_st = split(v).astype(BF16)
        sa_st = _dot(t_inv.astype(BF16), (xs[:2 * L] + _dot(m_ak.astype(BF16), v_st)).astype(BF16))
        y_st = xs[2 * L:] + _dot(jnp.concatenate([m_rb, m_rk], axis=1).astype(BF16),
                                 jnp.concatenate([sa_st.astype(BF16), v_st], axis=0))
        y = y_st[:L] + y_st[L:]
        sa = sa_st[:L] + sa_st[L:]
        ds = _dot_tn(jnp.concatenate([sa, v], axis=0).astype(BF16),
                     jnp.concatenate([bt, kt], axis=0).astype(BF16))
        state[p] = jnp.where(block_diag, s_bd + ds, 0.0) * p_in[L - 1:L, :]

        mean = head_sum(y) * (1.0 / HEAD)
        dev = y - mean
        var = head_sum(dev * dev) * (1.0 / HEAD)
        yn = dev * lax.rsqrt(var + RWKV_GN_EPS) * lw_ref[:, sl] + lb_ref[:, sl]
        bonus = head_sum(r * kx * rk_ref[:, sl]) * v
        o_ref[:, sl] = ((yn + bonus) * g_ref[:, sl]).astype(BF16)


def _rwkv_rec(r, ld, k, v, kk, a, g, r_k, lnx_w, lnx_b, batch, seq, pairs):
    rows, d = r.shape
    width = pairs * GROUP
    groups = d // width
    nc = seq // RWKV_L
    spec = pl.BlockSpec((RWKV_L, width), lambda b, p, c: (b * nc + c, p))
    vec = pl.BlockSpec((1, width), lambda b, p, c: (0, p))
    return pl.pallas_call(
        _rwkv_rec_kernel,
        grid=(batch, groups, nc),
        in_specs=[spec] * 7 + [vec] * 3,
        out_specs=spec,
        out_shape=jax.ShapeDtypeStruct((rows, d), BF16),
        scratch_shapes=[pltpu.VMEM((pairs, GROUP, GROUP), F32)],
        compiler_params=_params(("parallel", "parallel", "arbitrary")),
        name="rwkv_recurrence",
    )(r, ld, k, v, kk, a, g, r_k, lnx_w, lnx_b)


def _rotary_tables(seq):
    inv_freq = 1.0 / (ROPE_THETA ** (jnp.arange(0, HEAD, 2, dtype=F32) / HEAD))
    ang = jnp.arange(seq, dtype=F32)[:, None] * inv_freq[None, :]
    cos, sin = jnp.cos(ang), jnp.sin(ang)
    cos = jnp.concatenate([cos, cos, cos, cos], axis=1)
    sin = jnp.concatenate([-sin, sin, -sin, sin], axis=1)
    return cos, sin


def _pad_cols(w):
    return jnp.pad(w, ((0, 0), (0, LORA_PAD - w.shape[1])))


def _pad_rows(w):
    return jnp.pad(w, ((0, LORA_PAD - w.shape[0]), (0, 0)))


def kernel(x, e_norm_g, e_w_in, e_pool_w, e_pool_scale, e_lambda, e_subln_g, e_w_o, o_norm_g, o_mu, o_w_r, o_w_k, o_w_v, o_w_o, o_w0, o_w1, o_w2, o_a0, o_a1, o_a2, o_g1, o_g2, o_k_k, o_k_a, o_r_k, o_lnx_w, o_lnx_b, f_norm_g, f_w_in, f_conv_w, f_conv_b, f_w_out, final_g):
    batch, seq, d = x.shape
    depth = f_norm_g.shape[0]
    d_ff = f_w_out.shape[1]
    tm = min(512, seq)
    attn_t = min(256, seq)
    x2 = x.reshape(batch * seq, d)
    row = lambda vec: vec.reshape(1, -1)
    cos, sin = _rotary_tables(seq)

    for layer in range(depth):
        i = layer // 2
        if layer % 2 == 0:
            lambda_init = 0.8 - 0.6 * math.exp(-0.3 * layer)
            width = e_pool_scale.shape[1]
            u, q, k, v = _even_proj(x2, row(e_norm_g[i]), e_w_in[i].astype(BF16), cos, sin, seq, tm)
            pool_out = _pool(u, e_pool_w[i].astype(BF16), row(e_pool_scale[i]), batch, seq, tm)
            attn_out = _diff_attention(q, k, v, e_lambda[i], row(e_subln_g[i]), batch, seq,
                                       lambda_init, attn_t)
            w_o = e_w_o[i].astype(BF16)
            x2 = _proj_res(x2, [pool_out, attn_out], [w_o[:width], w_o[width:]], tm)
        else:
            r, ld, k, v, kk, a, g = _rwkv_proj(
                x2, row(o_norm_g[i]), o_mu[i], o_w_r[i].astype(BF16), o_w_k[i].astype(BF16),
                o_w_v[i].astype(BF16), row(o_w0[i]), _pad_cols(o_w1[i]).astype(BF16),
                _pad_rows(o_w2[i]).astype(BF16), row(o_a0[i]), _pad_cols(o_a1[i]).astype(BF16),
                _pad_rows(o_a2[i]).astype(BF16), o_g1[i].astype(BF16), o_g2[i].astype(BF16),
                row(o_k_k[i]), row(o_k_a[i]), seq, min(256, seq))
            z = _rwkv_rec(r, ld, k, v, kk, a, g, row(o_r_k[i]), row(o_lnx_w[i]), row(o_lnx_b[i]),
                          batch, seq, pairs=4)
            x2 = _proj_res(x2, [z], [o_w_o[i].astype(BF16)], tm)
        w_in = f_w_in[layer].astype(BF16)
        x2 = _ffn(x2, row(f_norm_g[layer]), w_in[:, :d_ff], w_in[:, d_ff:], f_conv_w[layer],
                  row(f_conv_b[layer]), f_w_out[layer].astype(BF16), row(final_g), seq, tm,
                  tf=256, final_norm=(layer == depth - 1))
    return x2.reshape(batch, seq, d)
```

```python
import functools
import math

import jax
import jax.numpy as jnp
from jax import lax
from jax.experimental import pallas as pl
from jax.experimental.pallas import tpu as pltpu

F32 = jnp.float32
BF16 = jnp.bfloat16

RMS_EPS = 1e-5
ROPE_THETA = 10000.0
CHUNK = 64
POOL_WINDOWS = (2, 4, 8, 16)
POOL_HALO = 16
GROUP = 128
HEAD = 64
RWKV_GN_EPS = 64e-5
RWKV_L = 64
LORA_PAD = 128
HALO = 8
NEG = -1e30
VMEM_LIMIT = 56 * 1024 * 1024


def _rms(x, g):
    return x * lax.rsqrt(jnp.mean(x * x, axis=-1, keepdims=True) + RMS_EPS) * g


def _dot(a, b):
    return jnp.dot(a, b, preferred_element_type=F32)


def _dot_nt(a, b):
    return lax.dot_general(a, b, (((1,), (1,)), ((), ())), preferred_element_type=F32)


def _dot_tn(a, b):
    return lax.dot_general(a, b, (((0,), (0,)), ((), ())), preferred_element_type=F32)


def _bdot(a, b):
    return lax.dot_general(a, b, (((2,), (1,)), ((0,), (0,))), preferred_element_type=F32)


def _bdot_nt(a, b):
    return lax.dot_general(a, b, (((2,), (2,)), ((0,), (0,))), preferred_element_type=F32)


def _bdot_tn(a, b):
    return lax.dot_general(a, b, (((1,), (1,)), ((0,), (0,))), preferred_element_type=F32)


def _const_spec(shape):
    return pl.BlockSpec(shape, lambda *_: (0,) * len(shape))


def _params(sem):
    return pltpu.CompilerParams(dimension_semantics=sem, vmem_limit_bytes=VMEM_LIMIT)


def _even_proj_kernel(x_ref, g_ref, w_ref, cos_ref, sin_ref, u_ref, q_ref, k_ref, v_ref):
    width = u_ref.shape[1]
    xn = _rms(x_ref[...], g_ref[...]).astype(BF16)
    cos = jnp.concatenate([cos_ref[...]] * (width // GROUP), axis=1)
    sin = jnp.concatenate([sin_ref[...]] * (width // GROUP), axis=1)
    lane = lax.broadcasted_iota(jnp.int32, cos.shape, 1)
    first_half = (lane % HEAD) < HEAD // 2

    def proj(idx):
        return _dot(xn, w_ref[:, idx * width:(idx + 1) * width])

    def rope(t):
        partner = jnp.where(first_half, pltpu.roll(t, width - HEAD // 2, 1),
                            pltpu.roll(t, HEAD // 2, 1))
        return t * cos + partner * sin

    u_ref[...] = proj(0)
    q_ref[...] = (rope(proj(1)) * (HEAD ** -0.5)).astype(BF16)
    k_ref[...] = rope(proj(2)).astype(BF16)
    v_ref[...] = proj(3).astype(BF16)


def _even_proj(x2, g, w, cos, sin, seq, tm):
    rows, d = x2.shape
    width = w.shape[1] // 4
    tiles_per_seq = seq // tm
    row_spec = pl.BlockSpec((tm, d), lambda i: (i, 0))
    out_spec = pl.BlockSpec((tm, width), lambda i: (i, 0))
    tab_spec = pl.BlockSpec((tm, GROUP), lambda i: (i % tiles_per_seq, 0))
    return pl.pallas_call(
        _even_proj_kernel,
        grid=(rows // tm,),
        in_specs=[row_spec, _const_spec((1, d)), _const_spec(w.shape), tab_spec, tab_spec],
        out_specs=[out_spec] * 4,
        out_shape=[jax.ShapeDtypeStruct((rows, width), F32)]
        + [jax.ShapeDtypeStruct((rows, width), BF16)] * 3,
        compiler_params=_params(("parallel",)),
        name="even_proj",
    )(x2, g, w, cos, sin)


def _pool_kernel(u_ref, pw_ref, ps_ref, o_ref, buf):
    tp = u_ref.shape[0]
    t = pl.program_id(1)

    @pl.when(t == 0)
    def _():
        buf[0:POOL_HALO, :] = jnp.zeros((POOL_HALO, buf.shape[1]), F32)

    u = u_ref[...]
    buf[POOL_HALO:, :] = u
    pos = t * tp + lax.broadcasted_iota(jnp.int32, (tp, 1), 0)
    for gi, win in enumerate(POOL_WINDOWS):
        sl = slice(gi * GROUP, (gi + 1) * GROUP)
        ug = u[:, sl]
        acc = ug
        for j in range(1, win):
            acc = acc + buf[POOL_HALO - j:POOL_HALO - j + tp, sl]
        count = jnp.minimum(pos + 1, win).astype(F32)
        delta = acc / count - ug
        y = _dot(delta.astype(BF16), pw_ref[gi]) * ps_ref[:, sl]
        o_ref[:, sl] = y.astype(BF16)
    buf[0:POOL_HALO, :] = buf[tp:tp + POOL_HALO, :]


def _pool(u, pool_w, pool_scale, batch, seq, tp):
    rows, width = u.shape
    nt = seq // tp
    spec = pl.BlockSpec((tp, width), lambda b, t: (b * nt + t, 0))
    return pl.pallas_call(
        _pool_kernel,
        grid=(batch, nt),
        in_specs=[spec, _const_spec(pool_w.shape), _const_spec((1, width))],
        out_specs=spec,
        out_shape=jax.ShapeDtypeStruct((rows, width), BF16),
        scratch_shapes=[pltpu.VMEM((POOL_HALO + tp, width), F32)],
        compiler_params=_params(("arbitrary", "arbitrary")),
        name="pool_mixer",
    )(u, pool_w, pool_scale)


def _attn_kernel(lam_ref, q_ref, k_ref, v_ref, sg_ref, o_ref, s_buf, *, lambda_init):
    t = q_ref.shape[0]
    i = pl.program_id(2)
    q = q_ref[...]
    lane = lax.broadcasted_iota(jnp.int32, q.shape, 1)
    zero = jnp.zeros_like(q)
    qs = jnp.concatenate([jnp.where(lane < HEAD, q, zero), jnp.where(lane < HEAD, zero, q)], axis=0)

    def lane_fold(op, run, s):
        for c in range(t // GROUP):
            run = op(run, s[:, c * GROUP:(c + 1) * GROUP])
        return run

    def scores(j):
        return _dot_nt(qs, k_ref[pl.ds(pl.multiple_of(j * t, t), t), :])

    def pass1(j, run):
        s = scores(j)
        s_buf[j] = s
        return lane_fold(jnp.maximum, run, s)

    run = lax.fori_loop(0, i, pass1, jnp.full((2 * t, GROUP), NEG, F32))
    s = scores(i)
    row = lax.broadcasted_iota(jnp.int32, s.shape, 0)
    col = lax.broadcasted_iota(jnp.int32, s.shape, 1)
    s = jnp.where((col // CHUNK) <= ((row % t) // CHUNK), s, NEG)
    s_buf[i] = s
    m = jnp.max(lane_fold(jnp.maximum, run, s), axis=-1, keepdims=True)

    def pass2(j, carry):
        lsum, acc = carry
        p = jnp.exp(s_buf[j] - m)
        vb = v_ref[pl.ds(pl.multiple_of(j * t, t), t), :]
        return lane_fold(jnp.add, lsum, p), acc + _dot(p.astype(BF16), vb)

    zeros = jnp.zeros((2 * t, GROUP), F32)
    lsum, acc = lax.fori_loop(0, i + 1, pass2, (zeros, zeros))
    o = acc / jnp.sum(lsum, axis=-1, keepdims=True)
    lv = lam_ref[...]
    lam = (jnp.exp(jnp.sum(lv[0:1] * lv[1:2], axis=-1, keepdims=True))
           - jnp.exp(jnp.sum(lv[2:3] * lv[3:4], axis=-1, keepdims=True)) + lambda_init)
    o = o[:t] - lam * o[t:]
    o = _rms(o, sg_ref[...]) * (1.0 - lambda_init)
    o_ref[...] = o.astype(BF16)


def _diff_attention(q, k, v, lam_vecs, subln_g, batch, seq, lambda_init, t):
    rows, width = q.shape
    heads = width // GROUP
    nq = seq // t
    q_spec = pl.BlockSpec((t, GROUP), lambda b, h, i: (b * nq + i, h))
    kv_spec = pl.BlockSpec((seq, GROUP), lambda b, h, i: (b, h))
    return pl.pallas_call(
        functools.partial(_attn_kernel, lambda_init=lambda_init),
        grid=(batch, heads, nq),
        in_specs=[_const_spec(lam_vecs.shape), q_spec, kv_spec, kv_spec, _const_spec((1, GROUP))],
        out_specs=q_spec,
        out_shape=jax.ShapeDtypeStruct((rows, width), BF16),
        scratch_shapes=[pltpu.VMEM((nq, 2 * t, t), F32)],
        compiler_params=_params(("parallel", "parallel", "arbitrary")),
        name="diff_attention",
    )(lam_vecs, q, k, v, subln_g)


def _proj_res_kernel(*refs, n_in):
    x_ref, o_ref = refs[0], refs[-1]
    acc = x_ref[...]
    for a_ref, w_ref in zip(refs[1:1 + n_in], refs[1 + n_in:1 + 2 * n_in]):
        acc = acc + _dot(a_ref[...], w_ref[...])
    o_ref[...] = acc


def _proj_res(x2, acts, weights, tm):
    rows, d = x2.shape
    in_specs = [pl.BlockSpec((tm, d), lambda i: (i, 0))]
    in_specs += [pl.BlockSpec((tm, a.shape[1]), lambda i: (i, 0)) for a in acts]
    in_specs += [_const_spec(w.shape) for w in weights]
    return pl.pallas_call(
        functools.partial(_proj_res_kernel, n_in=len(acts)),
        grid=(rows // tm,),
        in_specs=in_specs,
        out_specs=pl.BlockSpec((tm, d), lambda i: (i, 0)),
        out_shape=jax.ShapeDtypeStruct((rows, d), F32),
        compiler_params=_params(("parallel",)),
        name="proj_residual",
    )(x2, *acts, *weights)


def _ffn_kernel(x_ref, halo_ref, g_ref, wa_ref, wg_ref, cw_ref, cb_ref, wo_ref, fg_ref, o_ref,
                abuf, hbuf, *, tiles_per_seq, tf, final_norm):
    tm = x_ref.shape[0]
    d_ff = hbuf.shape[1]
    x = x_ref[...]
    g = g_ref[...]
    xn = _rms(x, g).astype(BF16)
    seq_start = (pl.program_id(0) % tiles_per_seq) == 0
    xh = jnp.where(seq_start, 0.0, _rms(halo_ref[...], g)).astype(BF16)

    def chunk(c, carry):
        col = pl.multiple_of(c * tf, tf)
        wa = wa_ref[:, pl.ds(col, tf)]
        abuf[0:HALO, :] = _dot(xh, wa)
        a = _dot(xn, wa)
        abuf[HALO:, :] = a
        gate = _dot(xn, wg_ref[:, pl.ds(col, tf)])
        cw = cw_ref[:, pl.ds(col, tf)]
        conv = (abuf[HALO - 2:HALO - 2 + tm, :] * cw[0:1] + abuf[HALO - 1:HALO - 1 + tm, :] * cw[1:2]
                + a * cw[2:3] + cb_ref[:, pl.ds(col, tf)])
        act = 0.5 * conv * (1.0 + lax.erf(conv * (2.0 ** -0.5)))
        hbuf[:, pl.ds(col, tf)] = (act * gate).astype(BF16)
        return carry

    lax.fori_loop(0, d_ff // tf, chunk, 0)
    out = x + _dot(hbuf[...], wo_ref[...])
    if final_norm:
        out = _rms(out, fg_ref[...])
    o_ref[...] = out


def _ffn(x2, g, wa, wg, conv_w, conv_b, w_out, final_g, seq, tm, tf, final_norm):
    rows, d = x2.shape
    d_ff = wa.shape[1]
    tiles_per_seq = seq // tm
    halo_blocks = tm // HALO
    row_spec = pl.BlockSpec((tm, d), lambda i: (i, 0))
    halo_spec = pl.BlockSpec((HALO, d), lambda i: (jnp.maximum(i * halo_blocks - 1, 0), 0))
    return pl.pallas_call(
        functools.partial(_ffn_kernel, tiles_per_seq=tiles_per_seq, tf=tf, final_norm=final_norm),
        grid=(rows // tm,),
        in_specs=[row_spec, halo_spec, _const_spec((1, d)), _const_spec(wa.shape),
                  _const_spec(wg.shape), _const_spec(conv_w.shape), _const_spec((1, d_ff)),
                  _const_spec(w_out.shape), _const_spec((1, d))],
        out_specs=row_spec,
        out_shape=jax.ShapeDtypeStruct((rows, d), F32),
        scratch_shapes=[pltpu.VMEM((HALO + tm, tf), F32), pltpu.VMEM((tm, d_ff), BF16)],
        compiler_params=_params(("parallel",)),
        name="conv_ffn",
    )(x2, x2, g, wa, wg, conv_w, conv_b, w_out, final_g)


def _rwkv_proj_kernel(x_ref, halo_ref, g_ref, mu_ref, wr_ref, wk_ref, wv_ref, w0_ref, w1_ref,
                      w2_ref, a0_ref, a1_ref, a2_ref, g1_ref, g2_ref, kk_ref, ka_ref,
                      r_out, ld_out, k_out, v_out, kk_out, a_out, g_out, hbuf, *, tiles_per_seq):
    tm = x_ref.shape[0]
    g = g_ref[...]
    h = _rms(x_ref[...], g)
    seq_start = (pl.program_id(0) % tiles_per_seq) == 0
    hbuf[0:HALO, :] = jnp.where(seq_start, 0.0, _rms(halo_ref[...], g))
    hbuf[HALO:, :] = h
    xx = hbuf[HALO - 1:HALO - 1 + tm, :] - h

    def mix(idx):
        return (h + xx * mu_ref[idx:idx + 1, :]).astype(BF16)

    r_out[...] = _dot(mix(0), wr_ref[...])
    w_lin = w0_ref[...] + _dot(jnp.tanh(_dot(mix(1), w1_ref[...])).astype(BF16), w2_ref[...])
    w_log = -jax.nn.softplus(-w_lin) - 0.5
    ld_out[...] = -jnp.exp(w_log)
    k = _dot(mix(2), wk_ref[...])
    v_out[...] = _dot(mix(3), wv_ref[...])
    a = jax.nn.sigmoid(a0_ref[...] + _dot(_dot(mix(4), a1_ref[...]).astype(BF16), a2_ref[...]))
    a_out[...] = a
    g_out[...] = _dot(jax.nn.sigmoid(_dot(mix(5), g1_ref[...])).astype(BF16), g2_ref[...])
    kk_out[...] = k * kk_ref[...]
    k_out[...] = k * (1.0 + (a - 1.0) * ka_ref[...])


def _rwkv_proj(x2, g, mu, wr, wk, wv, w0, w1, w2, a0, a1, a2, g1, g2, k_k, k_a, seq, tm):
    rows, d = x2.shape
    tiles_per_seq = seq // tm
    halo_blocks = tm // HALO
    row_spec = pl.BlockSpec((tm, d), lambda i: (i, 0))
    halo_spec = pl.BlockSpec((HALO, d), lambda i: (jnp.maximum(i * halo_blocks - 1, 0), 0))
    vec = _const_spec((1, d))
    consts = [mu, wr, wk, wv, w0, w1, w2, a0, a1, a2, g1, g2, k_k, k_a]
    return pl.pallas_call(
        functools.partial(_rwkv_proj_kernel, tiles_per_seq=tiles_per_seq),
        grid=(rows // tm,),
        in_specs=[row_spec, halo_spec, vec] + [_const_spec(c.shape) for c in consts],
        out_specs=[row_spec] * 7,
        out_shape=[jax.ShapeDtypeStruct((rows, d), F32)] * 7,
        scratch_shapes=[pltpu.VMEM((HALO + tm, d), F32)],
        compiler_params=_params(("parallel",)),
        name="rwkv_proj",
    )(x2, x2, g, *consts)


def _rwkv_rec_kernel(r_ref, ld_ref, k_ref, v_ref, kk_ref, a_ref, g_ref, rk_ref, lw_ref, lb_ref,
                     o_ref, state):
    L = r_ref.shape[0]
    pairs = r_ref.shape[1] // GROUP

    @pl.when(pl.program_id(2) == 0)
    def _():
        state[...] = jnp.zeros(state.shape, F32)

    def stack(t):
        return jnp.stack([t[:, p * GROUP:(p + 1) * GROUP] for p in range(pairs)], axis=0)

    head0 = lax.broadcasted_iota(jnp.int32, (pairs, L, GROUP), 2) < HEAD

    def split(t):
        return jnp.concatenate([jnp.where(head0, t, 0.0), jnp.where(head0, 0.0, t)], axis=1)

    def head_sum(t):
        s0 = jnp.sum(jnp.where(head0, t, 0.0), axis=-1, keepdims=True)
        s1 = jnp.sum(jnp.where(head0, 0.0, t), axis=-1, keepdims=True)
        return jnp.where(head0, s0, s1)

    row = lax.broadcasted_iota(jnp.int32, (pairs, 4 * L, 2 * L), 1)
    col = lax.broadcasted_iota(jnp.int32, (pairs, 4 * L, 2 * L), 2)
    same_head = ((row // L) % 2) == (col // L)
    strict = (col % L) < (row % L)
    keep = same_head & (strict | ((row >= 2 * L) & ((col % L) == (row % L))))
    r2 = lax.broadcasted_iota(jnp.int32, (pairs, 2 * L, 2 * L), 1)
    c2 = lax.broadcasted_iota(jnp.int32, (pairs, 2 * L, 2 * L), 2)
    eye = (r2 == c2).astype(F32)
    block_diag = (r2 // HEAD) == (c2 // HEAD)
    tl = lax.broadcasted_iota(jnp.int32, (L, L), 0)
    tc = lax.broadcasted_iota(jnp.int32, (L, L), 1)
    tri = (tc <= tl).astype(BF16)

    ld2 = ld_ref[...]
    width = ld2.shape[1]
    ld_hi = ld2.astype(BF16)
    rem = ld2 - ld_hi.astype(F32)
    ld_mid = rem.astype(BF16)
    ld_lo = (rem - ld_mid.astype(F32)).astype(BF16)
    c3 = _dot(tri, jnp.concatenate([ld_hi, ld_mid, ld_lo], axis=1))
    cum = stack(c3[:, :width] + c3[:, width:2 * width] + c3[:, 2 * width:])

    r, ld, kx, v = stack(r_ref[...]), stack(ld2), stack(k_ref[...]), stack(v_ref[...])
    kk, a = stack(kk_ref[...]), stack(a_ref[...])
    kkn = kk / jnp.maximum(jnp.sqrt(head_sum(kk * kk)), 1e-12)
    p_in = jnp.exp(cum)
    inv_p = jnp.exp(-cum)
    rt = r * p_in
    at = -kkn * jnp.exp(cum - ld)
    bt = kkn * a * inv_p
    kt = kx * inv_p

    lhs = jnp.concatenate([split(at), split(rt)], axis=1).astype(BF16)
    xb = jnp.where(keep, _bdot_nt(lhs, jnp.concatenate([bt, bt], axis=1).astype(BF16)), 0.0)
    xk = jnp.where(keep, _bdot_nt(lhs, jnp.concatenate([kt, kt], axis=1).astype(BF16)), 0.0)
    m_ab, m_rb = xb[:, :2 * L], xb[:, 2 * L:]
    m_ak, m_rk = xk[:, :2 * L], xk[:, 2 * L:]

    n = m_ab.astype(BF16)
    t_inv = eye + m_ab
    for _ in range(int(math.log2(L)) - 1):
        n = _bdot(n, n).astype(BF16)
        t_inv = t_inv + _bdot(t_inv.astype(BF16), n)

    s_bd = state[...]
    xs = _bdot_nt(lhs, s_bd.astype(BF16))
    v_st = split(v).astype(BF16)
    sa_st = _bdot(t_inv.astype(BF16), (xs[:, :2 * L] + _bdot(m_ak.astype(BF16), v_st)).astype(BF16))
    y_st = xs[:, 2 * L:] + _bdot(jnp.concatenate([m_rb, m_rk], axis=2).astype(BF16),
                                 jnp.concatenate([sa_st.astype(BF16), v_st], axis=1))
    y = y_st[:, :L] + y_st[:, L:]
    sa = sa_st[:, :L] + sa_st[:, L:]
    ds = _bdot_tn(jnp.concatenate([sa, v], axis=1).astype(BF16),
                  jnp.concatenate([bt, kt], axis=1).astype(BF16))
    state[...] = jnp.where(block_diag, s_bd + ds, 0.0) * p_in[:, L - 1:L, :]

    mean = head_sum(y) * (1.0 / HEAD)
    dev = y - mean
    var = head_sum(dev * dev) * (1.0 / HEAD)
    yn = dev * lax.rsqrt(var + RWKV_GN_EPS) * stack(lw_ref[...]) + stack(lb_ref[...])
    bonus = head_sum(r * kx * stack(rk_ref[...])) * v
    out = ((yn + bonus) * stack(g_ref[...])).astype(BF16)
    for p in range(pairs):
        o_ref[:, p * GROUP:(p + 1) * GROUP] = out[p]


def _rwkv_rec(r, ld, k, v, kk, a, g, r_k, lnx_w, lnx_b, batch, seq, pairs):
    rows, d = r.shape
    width = pairs * GROUP
    groups = d // width
    nc = seq // RWKV_L
    spec = pl.BlockSpec((RWKV_L, width), lambda b, p, c: (b * nc + c, p))
    vec = pl.BlockSpec((1, width), lambda b, p, c: (0, p))
    return pl.pallas_call(
        _rwkv_rec_kernel,
        grid=(batch, groups, nc),
        in_specs=[spec] * 7 + [vec] * 3,
        out_specs=spec,
        out_shape=jax.ShapeDtypeStruct((rows, d), BF16),
        scratch_shapes=[pltpu.VMEM((pairs, GROUP, GROUP), F32)],
        compiler_params=_params(("parallel", "parallel", "arbitrary")),
        name="rwkv_recurrence",
    )(r, ld, k, v, kk, a, g, r_k, lnx_w, lnx_b)


def _rotary_tables(seq):
    inv_freq = 1.0 / (ROPE_THETA ** (jnp.arange(0, HEAD, 2, dtype=F32) / HEAD))
    ang = jnp.arange(seq, dtype=F32)[:, None] * inv_freq[None, :]
    cos, sin = jnp.cos(ang), jnp.sin(ang)
    cos = jnp.concatenate([cos, cos, cos, cos], axis=1)
    sin = jnp.concatenate([-sin, sin, -sin, sin], axis=1)
    return cos, sin


def _pad_cols(w):
    return jnp.pad(w, ((0, 0), (0, LORA_PAD - w.shape[1])))


def _pad_rows(w):
    return jnp.pad(w, ((0, LORA_PAD - w.shape[0]), (0, 0)))


def kernel(x, e_norm_g, e_w_in, e_pool_w, e_pool_scale, e_lambda, e_subln_g, e_w_o, o_norm_g, o_mu, o_w_r, o_w_k, o_w_v, o_w_o, o_w0, o_w1, o_w2, o_a0, o_a1, o_a2, o_g1, o_g2, o_k_k, o_k_a, o_r_k, o_lnx_w, o_lnx_b, f_norm_g, f_w_in, f_conv_w, f_conv_b, f_w_out, final_g):
    batch, seq, d = x.shape
    depth = f_norm_g.shape[0]
    d_ff = f_w_out.shape[1]
    tm = min(512, seq)
    attn_t = min(512, seq)
    x2 = x.reshape(batch * seq, d)
    row = lambda vec: vec.reshape(1, -1)
    cos, sin = _rotary_tables(seq)

    for layer in range(depth):
        i = layer // 2
        if layer % 2 == 0:
            lambda_init = 0.8 - 0.6 * math.exp(-0.3 * layer)
            width = e_pool_scale.shape[1]
            u, q, k, v = _even_proj(x2, row(e_norm_g[i]), e_w_in[i].astype(BF16), cos, sin, seq, tm)
            pool_out = _pool(u, e_pool_w[i].astype(BF16), row(e_pool_scale[i]), batch, seq, tm)
            attn_out = _diff_attention(q, k, v, e_lambda[i], row(e_subln_g[i]), batch, seq,
                                       lambda_init, attn_t)
            w_o = e_w_o[i].astype(BF16)
            x2 = _proj_res(x2, [pool_out, attn_out], [w_o[:width], w_o[width:]], tm)
        else:
            r, ld, k, v, kk, a, g = _rwkv_proj(
                x2, row(o_norm_g[i]), o_mu[i], o_w_r[i].astype(BF16), o_w_k[i].astype(BF16),
                o_w_v[i].astype(BF16), row(o_w0[i]), _pad_cols(o_w1[i]).astype(BF16),
                _pad_rows(o_w2[i]).astype(BF16), row(o_a0[i]), _pad_cols(o_a1[i]).astype(BF16),
                _pad_rows(o_a2[i]).astype(BF16), o_g1[i].astype(BF16), o_g2[i].astype(BF16),
                row(o_k_k[i]), row(o_k_a[i]), seq, min(256, seq))
            z = _rwkv_rec(r, ld, k, v, kk, a, g, row(o_r_k[i]), row(o_lnx_w[i]), row(o_lnx_b[i]),
                          batch, seq, pairs=8)
            x2 = _proj_res(x2, [z], [o_w_o[i].astype(BF16)], tm)
        w_in = f_w_in[layer].astype(BF16)
        x2 = _ffn(x2, row(f_norm_g[layer]), w_in[:, :d_ff], w_in[:, d_ff:], f_conv_w[layer],
                  row(f_conv_b[layer]), f_w_out[layer].astype(BF16), row(final_g), seq, tm,
                  tf=256, final_norm=(layer == depth - 1))
    return x2.reshape(batch, seq, d)
```

```python
import functools
import math

import jax
import jax.numpy as jnp
from jax import lax
from jax.experimental import pallas as pl
from jax.experimental.pallas import tpu as pltpu

F32 = jnp.float32
BF16 = jnp.bfloat16

RMS_EPS = 1e-5
ROPE_THETA = 10000.0
CHUNK = 64
POOL_WINDOWS = (2, 4, 8, 16)
POOL_HALO = 16
GROUP = 128
HEAD = 64
RWKV_GN_EPS = 64e-5
RWKV_L = 64
LORA_PAD = 128
HALO = 8
NEG = -1e30
VMEM_LIMIT = 56 * 1024 * 1024


def _rms(x, g):
    return x * lax.rsqrt(jnp.mean(x * x, axis=-1, keepdims=True) + RMS_EPS) * g


def _dot(a, b):
    return jnp.dot(a, b, preferred_element_type=F32)


def _dot_nt(a, b):
    return lax.dot_general(a, b, (((1,), (1,)), ((), ())), preferred_element_type=F32)


def _dot_tn(a, b):
    return lax.dot_general(a, b, (((0,), (0,)), ((), ())), preferred_element_type=F32)


def _bdot(a, b):
    return lax.dot_general(a, b, (((2,), (1,)), ((0,), (0,))), preferred_element_type=F32)


def _bdot_nt(a, b):
    return lax.dot_general(a, b, (((2,), (2,)), ((0,), (0,))), preferred_element_type=F32)


def _bdot_tn(a, b):
    return lax.dot_general(a, b, (((1,), (1,)), ((0,), (0,))), preferred_element_type=F32)


def _shift_rows(t, halo, n):
    rolled = pltpu.roll(t, n, 0)
    row = lax.broadcasted_iota(jnp.int32, halo.shape, 0)
    head = jnp.where(row < n, pltpu.roll(halo, n, 0), rolled[:HALO])
    return jnp.concatenate([head, rolled[HALO:]], axis=0)


def _const_spec(shape):
    return pl.BlockSpec(shape, lambda *_: (0,) * len(shape))


def _params(sem):
    return pltpu.CompilerParams(dimension_semantics=sem, vmem_limit_bytes=VMEM_LIMIT)


def _even_proj_kernel(x_ref, g_ref, w_ref, cos_ref, sin_ref, u_ref, q_ref, k_ref, v_ref):
    width = u_ref.shape[1]
    xn = _rms(x_ref[...], g_ref[...]).astype(BF16)
    cos = jnp.concatenate([cos_ref[...]] * (width // GROUP), axis=1)
    sin = jnp.concatenate([sin_ref[...]] * (width // GROUP), axis=1)
    lane = lax.broadcasted_iota(jnp.int32, cos.shape, 1)
    first_half = (lane % HEAD) < HEAD // 2

    def proj(idx):
        return _dot(xn, w_ref[:, idx * width:(idx + 1) * width])

    def rope(t):
        partner = jnp.where(first_half, pltpu.roll(t, width - HEAD // 2, 1),
                            pltpu.roll(t, HEAD // 2, 1))
        return t * cos + partner * sin

    u_ref[...] = proj(0)
    q_ref[...] = (rope(proj(1)) * (HEAD ** -0.5)).astype(BF16)
    k_ref[...] = rope(proj(2)).astype(BF16)
    v_ref[...] = proj(3).astype(BF16)


def _even_proj(x2, g, w, cos, sin, seq, tm):
    rows, d = x2.shape
    width = w.shape[1] // 4
    tiles_per_seq = seq // tm
    row_spec = pl.BlockSpec((tm, d), lambda i: (i, 0))
    out_spec = pl.BlockSpec((tm, width), lambda i: (i, 0))
    tab_spec = pl.BlockSpec((tm, GROUP), lambda i: (i % tiles_per_seq, 0))
    return pl.pallas_call(
        _even_proj_kernel,
        grid=(rows // tm,),
        in_specs=[row_spec, _const_spec((1, d)), _const_spec(w.shape), tab_spec, tab_spec],
        out_specs=[out_spec] * 4,
        out_shape=[jax.ShapeDtypeStruct((rows, width), F32)]
        + [jax.ShapeDtypeStruct((rows, width), BF16)] * 3,
        compiler_params=_params(("parallel",)),
        name="even_proj",
    )(x2, g, w, cos, sin)


def _pool_kernel(u_ref, pw_ref, ps_ref, o_ref, buf):
    tp = u_ref.shape[0]
    t = pl.program_id(1)

    @pl.when(t == 0)
    def _():
        buf[0:POOL_HALO, :] = jnp.zeros((POOL_HALO, buf.shape[1]), F32)

    u = u_ref[...]
    buf[POOL_HALO:, :] = u
    pos = t * tp + lax.broadcasted_iota(jnp.int32, (tp, 1), 0)
    for gi, win in enumerate(POOL_WINDOWS):
        sl = slice(gi * GROUP, (gi + 1) * GROUP)
        ug = u[:, sl]
        acc = ug
        for j in range(1, win):
            acc = acc + buf[POOL_HALO - j:POOL_HALO - j + tp, sl]
        count = jnp.minimum(pos + 1, win).astype(F32)
        delta = acc / count - ug
        y = _dot(delta.astype(BF16), pw_ref[gi]) * ps_ref[:, sl]
        o_ref[:, sl] = y.astype(BF16)
    buf[0:POOL_HALO, :] = buf[tp:tp + POOL_HALO, :]


def _pool(u, pool_w, pool_scale, batch, seq, tp):
    rows, width = u.shape
    nt = seq // tp
    spec = pl.BlockSpec((tp, width), lambda b, t: (b * nt + t, 0))
    return pl.pallas_call(
        _pool_kernel,
        grid=(batch, nt),
        in_specs=[spec, _const_spec(pool_w.shape), _const_spec((1, width))],
        out_specs=spec,
        out_shape=jax.ShapeDtypeStruct((rows, width), BF16),
        scratch_shapes=[pltpu.VMEM((POOL_HALO + tp, width), F32)],
        compiler_params=_params(("arbitrary", "arbitrary")),
        name="pool_mixer",
    )(u, pool_w, pool_scale)


def _attn_kernel(lam_ref, q_ref, k_ref, v_ref, sg_ref, o_ref, s_buf, *, lambda_init):
    t = q_ref.shape[0]
    i = pl.program_id(2)
    q = q_ref[...]
    lane = lax.broadcasted_iota(jnp.int32, q.shape, 1)
    zero = jnp.zeros_like(q)
    qs = jnp.concatenate([jnp.where(lane < HEAD, q, zero), jnp.where(lane < HEAD, zero, q)], axis=0)

    def lane_fold(op, run, s):
        for c in range(t // GROUP):
            run = op(run, s[:, c * GROUP:(c + 1) * GROUP])
        return run

    def tile(nblk):
        run = jnp.full((2 * t, GROUP), NEG, F32)
        for j in range(nblk):
            kb = k_ref[j * t:(j + 1) * t, :]
            s = jnp.concatenate([_dot_nt(qs[:t], kb), _dot_nt(qs[t:], kb)], axis=0)
            if j == nblk - 1:
                row = lax.broadcasted_iota(jnp.int32, s.shape, 0)
                col = lax.broadcasted_iota(jnp.int32, s.shape, 1)
                s = jnp.where((col // CHUNK) <= ((row % t) // CHUNK), s, NEG)
            s_buf[j] = s
            run = lane_fold(jnp.maximum, run, s)
        m = jnp.max(run, axis=-1, keepdims=True)

        lsum = jnp.zeros((2 * t, GROUP), F32)
        acc = jnp.zeros((2 * t, GROUP), F32)
        for j in range(nblk):
            p = jnp.exp(s_buf[j] - m)
            lsum = lane_fold(jnp.add, lsum, p)
            pb, vb = p.astype(BF16), v_ref[j * t:(j + 1) * t, :]
            acc = acc + jnp.concatenate([_dot(pb[:t], vb), _dot(pb[t:], vb)], axis=0)
        o = acc / jnp.sum(lsum, axis=-1, keepdims=True)
        lv = lam_ref[...]
        lam = (jnp.exp(jnp.sum(lv[0:1] * lv[1:2], axis=-1, keepdims=True))
               - jnp.exp(jnp.sum(lv[2:3] * lv[3:4], axis=-1, keepdims=True)) + lambda_init)
        o = o[:t] - lam * o[t:]
        o = _rms(o, sg_ref[...]) * (1.0 - lambda_init)
        o_ref[...] = o.astype(BF16)

    for c in range(s_buf.shape[0]):
        pl.when(i == c)(functools.partial(tile, c + 1))


def _diff_attention(q, k, v, lam_vecs, subln_g, batch, seq, lambda_init, t):
    rows, width = q.shape
    heads = width // GROUP
    nq = seq // t
    q_spec = pl.BlockSpec((t, GROUP), lambda b, h, i: (b * nq + i, h))
    kv_spec = pl.BlockSpec((seq, GROUP), lambda b, h, i: (b, h))
    return pl.pallas_call(
        functools.partial(_attn_kernel, lambda_init=lambda_init),
        grid=(batch, heads, nq),
        in_specs=[_const_spec(lam_vecs.shape), q_spec, kv_spec, kv_spec, _const_spec((1, GROUP))],
        out_specs=q_spec,
        out_shape=jax.ShapeDtypeStruct((rows, width), BF16),
        scratch_shapes=[pltpu.VMEM((nq, 2 * t, t), F32)],
        compiler_params=_params(("parallel", "parallel", "arbitrary")),
        name="diff_attention",
    )(lam_vecs, q, k, v, subln_g)


def _proj_res_kernel(*refs, n_in):
    x_ref, o_ref = refs[0], refs[-1]
    acc = x_ref[...]
    for a_ref, w_ref in zip(refs[1:1 + n_in], refs[1 + n_in:1 + 2 * n_in]):
        acc = acc + _dot(a_ref[...], w_ref[...])
    o_ref[...] = acc


def _proj_res(x2, acts, weights, tm):
    rows, d = x2.shape
    in_specs = [pl.BlockSpec((tm, d), lambda i: (i, 0))]
    in_specs += [pl.BlockSpec((tm, a.shape[1]), lambda i: (i, 0)) for a in acts]
    in_specs += [_const_spec(w.shape) for w in weights]
    return pl.pallas_call(
        functools.partial(_proj_res_kernel, n_in=len(acts)),
        grid=(rows // tm,),
        in_specs=in_specs,
        out_specs=pl.BlockSpec((tm, d), lambda i: (i, 0)),
        out_shape=jax.ShapeDtypeStruct((rows, d), F32),
        compiler_params=_params(("parallel",)),
        name="proj_residual",
    )(x2, *acts, *weights)


def _ffn_kernel(x_ref, halo_ref, g_ref, wa_ref, wg_ref, cw_ref, cb_ref, wo_ref, fg_ref, o_ref,
                abuf, hbuf, *, tiles_per_seq, tf, final_norm):
    tm = x_ref.shape[0]
    d_ff = hbuf.shape[1]
    x = x_ref[...]
    g = g_ref[...]
    xn = _rms(x, g).astype(BF16)
    seq_start = (pl.program_id(0) % tiles_per_seq) == 0
    xh = jnp.where(seq_start, 0.0, _rms(halo_ref[...], g)).astype(BF16)

    for c in range(d_ff // tf):
        cols = slice(c * tf, (c + 1) * tf)
        slot = c % 2
        wa = wa_ref[:, cols]
        abuf[slot, 0:HALO, :] = _dot(xh, wa)
        a = _dot(xn, wa)
        abuf[slot, HALO:, :] = a
        gate = _dot(xn, wg_ref[:, cols])
        cw = cw_ref[:, cols]
        conv = (abuf[slot, HALO - 2:HALO - 2 + tm, :] * cw[0:1]
                + abuf[slot, HALO - 1:HALO - 1 + tm, :] * cw[1:2] + a * cw[2:3] + cb_ref[:, cols])
        act = 0.5 * conv * (1.0 + lax.erf(conv * (2.0 ** -0.5)))
        hbuf[:, cols] = (act * gate).astype(BF16)
    out = x + _dot(hbuf[...], wo_ref[...])
    if final_norm:
        out = _rms(out, fg_ref[...])
    o_ref[...] = out


def _ffn(x2, g, wa, wg, conv_w, conv_b, w_out, final_g, seq, tm, tf, final_norm):
    rows, d = x2.shape
    d_ff = wa.shape[1]
    tiles_per_seq = seq // tm
    halo_blocks = tm // HALO
    row_spec = pl.BlockSpec((tm, d), lambda i: (i, 0))
    halo_spec = pl.BlockSpec((HALO, d), lambda i: (jnp.maximum(i * halo_blocks - 1, 0), 0))
    return pl.pallas_call(
        functools.partial(_ffn_kernel, tiles_per_seq=tiles_per_seq, tf=tf, final_norm=final_norm),
        grid=(rows // tm,),
        in_specs=[row_spec, halo_spec, _const_spec((1, d)), _const_spec(wa.shape),
                  _const_spec(wg.shape), _const_spec(conv_w.shape), _const_spec((1, d_ff)),
                  _const_spec(w_out.shape), _const_spec((1, d))],
        out_specs=row_spec,
        out_shape=jax.ShapeDtypeStruct((rows, d), F32),
        scratch_shapes=[pltpu.VMEM((2, HALO + tm, tf), F32), pltpu.VMEM((tm, d_ff), BF16)],
        compiler_params=_params(("parallel",)),
        name="conv_ffn",
    )(x2, x2, g, wa, wg, conv_w, conv_b, w_out, final_g)


def _rwkv_proj_kernel(x_ref, halo_ref, g_ref, mu_ref, wr_ref, wk_ref, wv_ref, w0_ref, w1_ref,
                      w2_ref, a0_ref, a1_ref, a2_ref, g1_ref, g2_ref, kk_ref, ka_ref,
                      r_out, ld_out, k_out, v_out, kk_out, a_out, g_out, *, tiles_per_seq, tn):
    g = g_ref[...]
    h = _rms(x_ref[...], g)
    seq_start = (pl.program_id(0) % tiles_per_seq) == 0
    h_halo = jnp.where(seq_start, 0.0, _rms(halo_ref[...], g))
    xx = _shift_rows(h, h_halo, 1) - h
    hb, xb, mu = h.astype(BF16), xx.astype(BF16), mu_ref[...].astype(BF16)

    def mix(idx):
        return hb + xb * mu[idx:idx + 1, :]

    xr, xk, xv = mix(0), mix(2), mix(3)
    w_hid = jnp.tanh(_dot(mix(1), w1_ref[...])).astype(BF16)
    a_hid = _dot(mix(4), a1_ref[...]).astype(BF16)
    g_hid = jax.nn.sigmoid(_dot(mix(5), g1_ref[...])).astype(BF16)
    for c in range(h.shape[1] // tn):
        cols = slice(c * tn, (c + 1) * tn)
        r_out[:, cols] = _dot(xr, wr_ref[:, cols]).astype(BF16)
        w_lin = w0_ref[:, cols] + _dot(w_hid, w2_ref[:, cols])
        w_log = -jax.nn.softplus(-w_lin) - 0.5
        ld_out[:, cols] = -jnp.exp(w_log)
        k = _dot(xk, wk_ref[:, cols])
        v_out[:, cols] = _dot(xv, wv_ref[:, cols]).astype(BF16)
        a = jax.nn.sigmoid(a0_ref[:, cols] + _dot(a_hid, a2_ref[:, cols]))
        a_out[:, cols] = a.astype(BF16)
        g_out[:, cols] = _dot(g_hid, g2_ref[:, cols]).astype(BF16)
        kk_out[:, cols] = (k * kk_ref[:, cols]).astype(BF16)
        k_out[:, cols] = (k * (1.0 + (a - 1.0) * ka_ref[:, cols])).astype(BF16)


def _rwkv_proj(x2, g, mu, wr, wk, wv, w0, w1, w2, a0, a1, a2, g1, g2, k_k, k_a, seq, tm):
    rows, d = x2.shape
    tiles_per_seq = seq // tm
    halo_blocks = tm // HALO
    row_spec = pl.BlockSpec((tm, d), lambda i: (i, 0))
    halo_spec = pl.BlockSpec((HALO, d), lambda i: (jnp.maximum(i * halo_blocks - 1, 0), 0))
    vec = _const_spec((1, d))
    consts = [mu, wr, wk, wv, w0, w1, w2, a0, a1, a2, g1, g2, k_k, k_a]
    return pl.pallas_call(
        functools.partial(_rwkv_proj_kernel, tiles_per_seq=tiles_per_seq, tn=256),
        grid=(rows // tm,),
        in_specs=[row_spec, halo_spec, vec] + [_const_spec(c.shape) for c in consts],
        out_specs=[row_spec] * 7,
        out_shape=[jax.ShapeDtypeStruct((rows, d), F32 if i == 1 else BF16) for i in range(7)],
        compiler_params=_params(("parallel",)),
        name="rwkv_proj",
    )(x2, x2, g, *consts)


def _rwkv_rec_kernel(r_ref, ld_ref, k_ref, v_ref, kk_ref, a_ref, g_ref, rk_ref, lw_ref, lb_ref,
                     o_ref, state):
    L = RWKV_L
    chunks = r_ref.shape[0] // L
    pairs = r_ref.shape[1] // GROUP

    @pl.when(pl.program_id(2) == 0)
    def _():
        state[...] = jnp.zeros(state.shape, F32)

    def stack(t):
        return jnp.stack([t[:, p * GROUP:(p + 1) * GROUP] for p in range(pairs)], axis=0)

    head0 = lax.broadcasted_iota(jnp.int32, (pairs, L, GROUP), 2) < HEAD

    def split(t):
        return jnp.concatenate([jnp.where(head0, t, 0.0), jnp.where(head0, 0.0, t)], axis=1)

    ones_r = lax.broadcasted_iota(jnp.int32, (GROUP, GROUP), 0) // HEAD
    ones_c = lax.broadcasted_iota(jnp.int32, (GROUP, GROUP), 1) // HEAD
    head_ones = (ones_r == ones_c).astype(BF16)

    def head_sum(t):
        flat = t.reshape(pairs * L, GROUP).astype(BF16)
        return _dot(flat, head_ones).reshape(pairs, L, GROUP)

    row = lax.broadcasted_iota(jnp.int32, (pairs, 4 * L, 2 * L), 1)
    col = lax.broadcasted_iota(jnp.int32, (pairs, 4 * L, 2 * L), 2)
    same_head = ((row // L) % 2) == (col // L)
    strict = (col % L) < (row % L)
    keep = same_head & (strict | ((row >= 2 * L) & ((col % L) == (row % L))))
    r2 = lax.broadcasted_iota(jnp.int32, (pairs, 2 * L, 2 * L), 1)
    c2 = lax.broadcasted_iota(jnp.int32, (pairs, 2 * L, 2 * L), 2)
    eye = (r2 == c2).astype(F32)
    block_diag = (r2 // HEAD) == (c2 // HEAD)
    tl = lax.broadcasted_iota(jnp.int32, (L, L), 0)
    tc = lax.broadcasted_iota(jnp.int32, (L, L), 1)
    tri = (tc <= tl).astype(BF16)

    lnx_w, lnx_b, r_k = stack(lw_ref[...]), stack(lb_ref[...]), stack(rk_ref[...])

    def prepare(rows):
        def load(ref):
            return stack(ref[rows, :]).astype(F32)

        ld2 = ld_ref[rows, :]
        width = ld2.shape[1]
        ld_hi = ld2.astype(BF16)
        rem = ld2 - ld_hi.astype(F32)
        ld_mid = rem.astype(BF16)
        ld_lo = (rem - ld_mid.astype(F32)).astype(BF16)
        c3 = _dot(tri, jnp.concatenate([ld_hi, ld_mid, ld_lo], axis=1))
        cum = stack(c3[:, :width] + c3[:, width:2 * width] + c3[:, 2 * width:])

        r, ld, kx, v, kk, a = load(r_ref), stack(ld2), load(k_ref), load(v_ref), load(kk_ref), load(a_ref)
        kkn = kk * jnp.minimum(lax.rsqrt(head_sum(kk * kk)), 1e12)
        p_in = jnp.exp(cum)
        inv_p = jnp.exp(-cum)
        rt = r * p_in
        at = -kkn * jnp.exp(cum - ld)
        bt = (kkn * a * inv_p).astype(BF16)
        kt = (kx * inv_p).astype(BF16)

        lhs = jnp.concatenate([at, rt], axis=1).astype(BF16)
        xbk = _bdot_nt(lhs, jnp.concatenate([split(bt), split(kt)], axis=1))
        xa, xr = xbk[:, :L], xbk[:, L:]
        xb = jnp.where(keep, jnp.concatenate([xa[:, :, :2 * L]] * 2 + [xr[:, :, :2 * L]] * 2, axis=1), 0.0)
        xk = jnp.where(keep, jnp.concatenate([xa[:, :, 2 * L:]] * 2 + [xr[:, :, 2 * L:]] * 2, axis=1), 0.0)
        bonus = head_sum(r * kx * r_k) * v
        return dict(m_ab=xb[:, :2 * L], m_ak=xk[:, :2 * L].astype(BF16),
                    m_r=jnp.concatenate([xb[:, 2 * L:], xk[:, 2 * L:]], axis=2).astype(BF16),
                    lhs=lhs, bk=jnp.concatenate([bt, kt], axis=1), v=v, v_st=split(v).astype(BF16),
                    p_last=p_in[:, L - 1:L, :], bonus=bonus, gate=load(g_ref))

    def invert(pre):
        m_ab = pre["m_ab"]
        n = _bdot(m_ab.astype(BF16), m_ab.astype(BF16))
        t_inv = eye + m_ab
        levels = int(math.log2(L))
        for level in range(1, levels):
            nb = n.astype(BF16)
            if level < levels - 1:
                prod = _bdot(nb, jnp.concatenate([nb, t_inv.astype(BF16)], axis=2))
                n, t_inv = prod[:, :, :2 * L], t_inv + prod[:, :, 2 * L:]
            else:
                t_inv = t_inv + _bdot(nb, t_inv.astype(BF16))
        return t_inv.astype(BF16), _bdot(pre["m_ak"], pre["v_st"])

    def advance(rows, pre, t_inv, akv, s_bd):
        xs = _bdot_nt(pre["lhs"], s_bd.astype(BF16))
        sa_st = _bdot(t_inv, (split(xs[:, :L]) + akv).astype(BF16))
        y_st = _bdot(pre["m_r"], jnp.concatenate([sa_st.astype(BF16), pre["v_st"]], axis=1))
        y = xs[:, L:] + y_st[:, :L] + y_st[:, L:]
        sa = sa_st[:, :L] + sa_st[:, L:]
        ds = _bdot_tn(jnp.concatenate([sa, pre["v"]], axis=1).astype(BF16), pre["bk"])
        s_new = jnp.where(block_diag, s_bd + ds, 0.0) * pre["p_last"]

        mean = head_sum(y) * (1.0 / HEAD)
        dev = y - mean
        var = head_sum(dev * dev) * (1.0 / HEAD)
        yn = dev * lax.rsqrt(var + RWKV_GN_EPS) * lnx_w + lnx_b
        out = ((yn + pre["bonus"]) * pre["gate"]).astype(BF16)
        for p in range(pairs):
            o_ref[rows, p * GROUP:(p + 1) * GROUP] = out[p]
        return s_new

    rows = [slice(c * L, (c + 1) * L) for c in range(chunks)]
    s_bd = state[...]
    pre = prepare(rows[0])
    for c in range(chunks):
        nxt = prepare(rows[c + 1]) if c + 1 < chunks else None
        t_inv, akv = invert(pre)
        s_bd = advance(rows[c], pre, t_inv, akv, s_bd)
        pre = nxt
    state[...] = s_bd


def _rwkv_rec(r, ld, k, v, kk, a, g, r_k, lnx_w, lnx_b, batch, seq, pairs, chunks):
    rows, d = r.shape
    width = pairs * GROUP
    groups = d // width
    nc = seq // (chunks * RWKV_L)
    spec = pl.BlockSpec((chunks * RWKV_L, width), lambda b, p, c: (b * nc + c, p))
    vec = pl.BlockSpec((1, width), lambda b, p, c: (0, p))
    return pl.pallas_call(
        _rwkv_rec_kernel,
        grid=(batch, groups, nc),
        in_specs=[spec] * 7 + [vec] * 3,
        out_specs=spec,
        out_shape=jax.ShapeDtypeStruct((rows, d), BF16),
        scratch_shapes=[pltpu.VMEM((pairs, GROUP, GROUP), F32)],
        compiler_params=_params(("parallel", "parallel", "arbitrary")),
        name="rwkv_recurrence",
    )(r, ld, k, v, kk, a, g, r_k, lnx_w, lnx_b)


def _rotary_tables(seq):
    inv_freq = 1.0 / (ROPE_THETA ** (jnp.arange(0, HEAD, 2, dtype=F32) / HEAD))
    ang = jnp.arange(seq, dtype=F32)[:, None] * inv_freq[None, :]
    cos, sin = jnp.cos(ang), jnp.sin(ang)
    cos = jnp.concatenate([cos, cos, cos, cos], axis=1)
    sin = jnp.concatenate([-sin, sin, -sin, sin], axis=1)
    return cos, sin


def _pad_cols(w):
    return jnp.pad(w, ((0, 0), (0, LORA_PAD - w.shape[1])))


def _pad_rows(w):
    return jnp.pad(w, ((0, LORA_PAD - w.shape[0]), (0, 0)))


def kernel(x, e_norm_g, e_w_in, e_pool_w, e_pool_scale, e_lambda, e_subln_g, e_w_o, o_norm_g, o_mu, o_w_r, o_w_k, o_w_v, o_w_o, o_w0, o_w1, o_w2, o_a0, o_a1, o_a2, o_g1, o_g2, o_k_k, o_k_a, o_r_k, o_lnx_w, o_lnx_b, f_norm_g, f_w_in, f_conv_w, f_conv_b, f_w_out, final_g):
    batch, seq, d = x.shape
    depth = f_norm_g.shape[0]
    d_ff = f_w_out.shape[1]
    tm = min(512, seq)
    attn_t = min(512, seq)
    x2 = x.reshape(batch * seq, d)
    row = lambda vec: vec.reshape(1, -1)
    cos, sin = _rotary_tables(seq)

    for layer in range(depth):
        i = layer // 2
        if layer % 2 == 0:
            lambda_init = 0.8 - 0.6 * math.exp(-0.3 * layer)
            width = e_pool_scale.shape[1]
            u, q, k, v = _even_proj(x2, row(e_norm_g[i]), e_w_in[i].astype(BF16), cos, sin, seq, tm)
            pool_out = _pool(u, e_pool_w[i].astype(BF16), row(e_pool_scale[i]), batch, seq, tm)
            attn_out = _diff_attention(q, k, v, e_lambda[i], row(e_subln_g[i]), batch, seq,
                                       lambda_init, attn_t)
            w_o = e_w_o[i].astype(BF16)
            x2 = _proj_res(x2, [pool_out, attn_out], [w_o[:width], w_o[width:]], tm)
        else:
            r, ld, k, v, kk, a, g = _rwkv_proj(
                x2, row(o_norm_g[i]), o_mu[i], o_w_r[i].astype(BF16), o_w_k[i].astype(BF16),
                o_w_v[i].astype(BF16), row(o_w0[i]), _pad_cols(o_w1[i]).astype(BF16),
                _pad_rows(o_w2[i]).astype(BF16), row(o_a0[i]), _pad_cols(o_a1[i]).astype(BF16),
                _pad_rows(o_a2[i]).astype(BF16), o_g1[i].astype(BF16), o_g2[i].astype(BF16),
                row(o_k_k[i]), row(o_k_a[i]), seq, min(256, seq))
            z = _rwkv_rec(r, ld, k, v, kk, a, g, row(o_r_k[i]), row(o_lnx_w[i]), row(o_lnx_b[i]),
                          batch, seq, pairs=8, chunks=4)
            x2 = _proj_res(x2, [z], [o_w_o[i].astype(BF16)], tm)
        w_in = f_w_in[layer].astype(BF16)
        x2 = _ffn(x2, row(f_norm_g[layer]), w_in[:, :d_ff], w_in[:, d_ff:], f_conv_w[layer],
                  row(f_conv_b[layer]), f_w_out[layer].astype(BF16), row(final_g), seq, tm,
                  tf=256, final_norm=(layer == depth - 1))
    return x2.reshape(batch, seq, d)
```

```python
import functools
import math

import jax
import jax.numpy as jnp
from jax import lax
from jax.experimental import pallas as pl
from jax.experimental.pallas import tpu as pltpu

F32 = jnp.float32
BF16 = jnp.bfloat16

RMS_EPS = 1e-5
ROPE_THETA = 10000.0
CHUNK = 64
POOL_WINDOWS = (2, 4, 8, 16)
POOL_HALO = 16
GROUP = 128
HEAD = 64
RWKV_GN_EPS = 64e-5
RWKV_L = 64
LORA_PAD = 128
HALO = 8
NEG = -1e30
LOG2_E = 1.4426950408889634
VMEM_LIMIT = 56 * 1024 * 1024


def _rms(x, g):
    return x * lax.rsqrt(jnp.mean(x * x, axis=-1, keepdims=True) + RMS_EPS) * g


def _dot(a, b):
    return jnp.dot(a, b, preferred_element_type=F32)


def _dot_nt(a, b):
    return lax.dot_general(a, b, (((1,), (1,)), ((), ())), preferred_element_type=F32)


def _dot_tn(a, b):
    return lax.dot_general(a, b, (((0,), (0,)), ((), ())), preferred_element_type=F32)


def _bdot(a, b):
    return lax.dot_general(a, b, (((2,), (1,)), ((0,), (0,))), preferred_element_type=F32)


def _bdot_nt(a, b):
    return lax.dot_general(a, b, (((2,), (2,)), ((0,), (0,))), preferred_element_type=F32)


def _bdot_tn(a, b):
    return lax.dot_general(a, b, (((1,), (1,)), ((0,), (0,))), preferred_element_type=F32)


def _shift_rows(t, halo, n):
    rolled = pltpu.roll(t, n, 0)
    row = lax.broadcasted_iota(jnp.int32, halo.shape, 0)
    head = jnp.where(row < n, pltpu.roll(halo, n, 0), rolled[:HALO])
    return jnp.concatenate([head, rolled[HALO:]], axis=0)


def _const_spec(shape):
    return pl.BlockSpec(shape, lambda *_: (0,) * len(shape), pipeline_mode=pl.Buffered(1))


def _params(sem):
    return pltpu.CompilerParams(dimension_semantics=sem, vmem_limit_bytes=VMEM_LIMIT)


def _even_proj_kernel(x_ref, g_ref, w_ref, cos_ref, sin_ref, u_ref, q_ref, k_ref, v_ref):
    width = u_ref.shape[1]
    xn = _rms(x_ref[...], g_ref[...]).astype(BF16)
    cos = jnp.concatenate([cos_ref[...]] * (width // GROUP), axis=1)
    sin = jnp.concatenate([sin_ref[...]] * (width // GROUP), axis=1)
    lane = lax.broadcasted_iota(jnp.int32, cos.shape, 1)
    first_half = (lane % HEAD) < HEAD // 2

    def proj(idx):
        return _dot(xn, w_ref[:, idx * width:(idx + 1) * width])

    def rope(t):
        partner = jnp.where(first_half, pltpu.roll(t, width - HEAD // 2, 1),
                            pltpu.roll(t, HEAD // 2, 1))
        return t * cos + partner * sin

    u_ref[...] = proj(0)
    q_ref[...] = (rope(proj(1)) * (HEAD ** -0.5 * LOG2_E)).astype(BF16)
    k_ref[...] = rope(proj(2)).astype(BF16)
    v_ref[...] = proj(3).astype(BF16)


def _even_proj(x2, g, w, cos, sin, seq, tm):
    rows, d = x2.shape
    width = w.shape[1] // 4
    tiles_per_seq = seq // tm
    row_spec = pl.BlockSpec((tm, d), lambda i: (i, 0))
    out_spec = pl.BlockSpec((tm, width), lambda i: (i, 0))
    tab_spec = pl.BlockSpec((tm, GROUP), lambda i: (i % tiles_per_seq, 0))
    return pl.pallas_call(
        _even_proj_kernel,
        grid=(rows // tm,),
        in_specs=[row_spec, _const_spec((1, d)), _const_spec(w.shape), tab_spec, tab_spec],
        out_specs=[out_spec] * 4,
        out_shape=[jax.ShapeDtypeStruct((rows, width), F32)]
        + [jax.ShapeDtypeStruct((rows, width), BF16)] * 3,
        compiler_params=_params(("parallel",)),
        name="even_proj",
    )(x2, g, w, cos, sin)


def _pool_kernel(u_ref, pw_ref, ps_ref, o_ref, buf):
    tp = u_ref.shape[0]
    t = pl.program_id(1)

    @pl.when(t == 0)
    def _():
        buf[0:POOL_HALO, :] = jnp.zeros((POOL_HALO, buf.shape[1]), F32)

    u = u_ref[...]
    buf[POOL_HALO:, :] = u
    pos = t * tp + lax.broadcasted_iota(jnp.int32, (tp, 1), 0)
    for gi, win in enumerate(POOL_WINDOWS):
        sl = slice(gi * GROUP, (gi + 1) * GROUP)
        ug = u[:, sl]
        acc = ug
        for j in range(1, win):
            acc = acc + buf[POOL_HALO - j:POOL_HALO - j + tp, sl]
        count = jnp.minimum(pos + 1, win).astype(F32)
        delta = acc / count - ug
        y = _dot(delta.astype(BF16), pw_ref[gi]) * ps_ref[:, sl]
        o_ref[:, sl] = y.astype(BF16)
    buf[0:POOL_HALO, :] = buf[tp:tp + POOL_HALO, :]


def _pool(u, pool_w, pool_scale, batch, seq, tp):
    rows, width = u.shape
    nt = seq // tp
    spec = pl.BlockSpec((tp, width), lambda b, t: (b * nt + t, 0))
    return pl.pallas_call(
        _pool_kernel,
        grid=(batch, nt),
        in_specs=[spec, _const_spec(pool_w.shape), _const_spec((1, width))],
        out_specs=spec,
        out_shape=jax.ShapeDtypeStruct((rows, width), BF16),
        scratch_shapes=[pltpu.VMEM((POOL_HALO + tp, width), F32)],
        compiler_params=_params(("arbitrary", "arbitrary")),
        name="pool_mixer",
    )(u, pool_w, pool_scale)


def _attn_kernel(lam_ref, q_ref, k_ref, v_ref, sg_ref, o_ref, s_buf, *, lambda_init):
    t = q_ref.shape[0]
    i = pl.program_id(2)
    q = q_ref[...]
    lane = lax.broadcasted_iota(jnp.int32, q.shape, 1)
    zero = jnp.zeros_like(q)
    qs = jnp.concatenate([jnp.where(lane < HEAD, q, zero), jnp.where(lane < HEAD, zero, q)], axis=0)

    def lane_fold(op, run, s):
        for c in range(t // GROUP):
            run = op(run, s[:, c * GROUP:(c + 1) * GROUP])
        return run

    def tile(nblk):
        run = jnp.full((2 * t, GROUP), NEG, F32)
        for j in range(nblk):
            kb = k_ref[j * t:(j + 1) * t, :]
            s = jnp.concatenate([_dot_nt(qs[:t], kb), _dot_nt(qs[t:], kb)], axis=0)
            if j == nblk - 1:
                row = lax.broadcasted_iota(jnp.int32, s.shape, 0)
                col = lax.broadcasted_iota(jnp.int32, s.shape, 1)
                s = jnp.where((col // CHUNK) <= ((row % t) // CHUNK), s, NEG)
            s_buf[j] = s
            run = lane_fold(jnp.maximum, run, s)
        m = jnp.max(run, axis=-1, keepdims=True)

        lsum = jnp.zeros((2 * t, GROUP), F32)
        acc = jnp.zeros((2 * t, GROUP), F32)
        for j in range(nblk):
            p = jnp.exp2(s_buf[j] - m)
            lsum = lane_fold(jnp.add, lsum, p)
            pb, vb = p.astype(BF16), v_ref[j * t:(j + 1) * t, :]
            acc = acc + jnp.concatenate([_dot(pb[:t], vb), _dot(pb[t:], vb)], axis=0)
        o = acc / jnp.sum(lsum, axis=-1, keepdims=True)
        lv = lam_ref[...]
        lam = (jnp.exp(jnp.sum(lv[0:1] * lv[1:2], axis=-1, keepdims=True))
               - jnp.exp(jnp.sum(lv[2:3] * lv[3:4], axis=-1, keepdims=True)) + lambda_init)
        o = o[:t] - lam * o[t:]
        o = _rms(o, sg_ref[...]) * (1.0 - lambda_init)
        o_ref[...] = o.astype(BF16)

    for c in range(s_buf.shape[0]):
        pl.when(i == c)(functools.partial(tile, c + 1))


def _diff_attention(q, k, v, lam_vecs, subln_g, batch, seq, lambda_init, t):
    rows, width = q.shape
    heads = width // GROUP
    nq = seq // t
    q_spec = pl.BlockSpec((t, GROUP), lambda b, h, i: (b * nq + i, h))
    kv_spec = pl.BlockSpec((seq, GROUP), lambda b, h, i: (b, h))
    return pl.pallas_call(
        functools.partial(_attn_kernel, lambda_init=lambda_init),
        grid=(batch, heads, nq),
        in_specs=[_const_spec(lam_vecs.shape), q_spec, kv_spec, kv_spec, _const_spec((1, GROUP))],
        out_specs=q_spec,
        out_shape=jax.ShapeDtypeStruct((rows, width), BF16),
        scratch_shapes=[pltpu.VMEM((nq, 2 * t, t), F32)],
        compiler_params=_params(("parallel", "parallel", "arbitrary")),
        name="diff_attention",
    )(lam_vecs, q, k, v, subln_g)


def _proj_res_kernel(*refs):
    x_ref, w_ref, o_ref = refs[0], refs[-2], refs[-1]
    acc = x_ref[...]
    row0 = 0
    for a_ref in refs[1:-2]:
        width = a_ref.shape[1]
        acc = acc + _dot(a_ref[...], w_ref[row0:row0 + width, :])
        row0 += width
    o_ref[...] = acc


def _proj_res(x2, acts, w, tm):
    rows, d = x2.shape
    in_specs = [pl.BlockSpec((tm, d), lambda i: (i, 0))]
    in_specs += [pl.BlockSpec((tm, a.shape[1]), lambda i: (i, 0)) for a in acts]
    in_specs += [_const_spec(w.shape)]
    return pl.pallas_call(
        _proj_res_kernel,
        grid=(rows // tm,),
        in_specs=in_specs,
        out_specs=pl.BlockSpec((tm, d), lambda i: (i, 0)),
        out_shape=jax.ShapeDtypeStruct((rows, d), F32),
        compiler_params=_params(("parallel",)),
        name="proj_residual",
    )(x2, *acts, w)


def _ffn_kernel(x_ref, halo_ref, g_ref, wi_ref, cw_ref, cb_ref, wo_ref, fg_ref, o_ref,
                abuf, hbuf, *, tiles_per_seq, tf, final_norm):
    tm = x_ref.shape[0]
    d_ff = hbuf.shape[1]
    x = x_ref[...]
    g = g_ref[...]
    xn = _rms(x, g).astype(BF16)
    seq_start = (pl.program_id(0) % tiles_per_seq) == 0
    xh = jnp.where(seq_start, 0.0, _rms(halo_ref[...], g)).astype(BF16)

    for c in range(d_ff // tf):
        cols = slice(c * tf, (c + 1) * tf)
        slot = c % 2
        wa = wi_ref[:, cols]
        abuf[slot, 0:HALO, :] = _dot(xh, wa)
        a = _dot(xn, wa)
        abuf[slot, HALO:, :] = a
        gate = _dot(xn, wi_ref[:, d_ff + c * tf:d_ff + (c + 1) * tf])
        cw = cw_ref[:, cols]
        conv = (abuf[slot, HALO - 2:HALO - 2 + tm, :] * cw[0:1]
                + abuf[slot, HALO - 1:HALO - 1 + tm, :] * cw[1:2] + a * cw[2:3] + cb_ref[:, cols])
        act = 0.5 * conv * (1.0 + lax.erf(conv * (2.0 ** -0.5)))
        hbuf[:, cols] = (act * gate).astype(BF16)
    out = x + _dot(hbuf[...], wo_ref[...])
    if final_norm:
        out = _rms(out, fg_ref[...])
    o_ref[...] = out


def _ffn(x2, g, w_in, conv_w, conv_b, w_out, final_g, seq, tm, tf, final_norm):
    rows, d = x2.shape
    d_ff = w_out.shape[0]
    tiles_per_seq = seq // tm
    halo_blocks = tm // HALO
    row_spec = pl.BlockSpec((tm, d), lambda i: (i, 0))
    halo_spec = pl.BlockSpec((HALO, d), lambda i: (jnp.maximum(i * halo_blocks - 1, 0), 0))
    return pl.pallas_call(
        functools.partial(_ffn_kernel, tiles_per_seq=tiles_per_seq, tf=tf, final_norm=final_norm),
        grid=(rows // tm,),
        in_specs=[row_spec, halo_spec, _const_spec((1, d)), _const_spec(w_in.shape),
                  _const_spec(conv_w.shape), _const_spec((1, d_ff)),
                  _const_spec(w_out.shape), _const_spec((1, d))],
        out_specs=row_spec,
        out_shape=jax.ShapeDtypeStruct((rows, d), F32),
        scratch_shapes=[pltpu.VMEM((2, HALO + tm, tf), F32), pltpu.VMEM((tm, d_ff), BF16)],
        compiler_params=_params(("parallel",)),
        name="conv_ffn",
    )(x2, x2, g, w_in, conv_w, conv_b, w_out, final_g)


def _rwkv_proj_kernel(x_ref, halo_ref, g_ref, mu_ref, wr_ref, wk_ref, wv_ref, w0_ref, w1_ref,
                      w2_ref, a0_ref, a1_ref, a2_ref, g1_ref, g2_ref, kk_ref, ka_ref,
                      r_out, ld_out, k_out, v_out, kk_out, a_out, g_out, *, tiles_per_seq, tn):
    g = g_ref[...]
    h = _rms(x_ref[...], g)
    seq_start = (pl.program_id(0) % tiles_per_seq) == 0
    h_halo = jnp.where(seq_start, 0.0, _rms(halo_ref[...], g))
    xx = _shift_rows(h, h_halo, 1) - h
    hb, xb, mu = h.astype(BF16), xx.astype(BF16), mu_ref[...].astype(BF16)

    def mix(idx):
        return hb + xb * mu[idx:idx + 1, :]

    xr, xk, xv = mix(0), mix(2), mix(3)
    w_hid = jnp.tanh(_dot(mix(1), w1_ref[...])).astype(BF16)
    a_hid = _dot(mix(4), a1_ref[...]).astype(BF16)
    g_hid = jax.nn.sigmoid(_dot(mix(5), g1_ref[...])).astype(BF16)
    for c in range(h.shape[1] // tn):
        cols = slice(c * tn, (c + 1) * tn)
        r_out[:, cols] = _dot(xr, wr_ref[:, cols]).astype(BF16)
        w_lin = w0_ref[:, cols] + _dot(w_hid, w2_ref[:, cols])
        w_log = -jax.nn.softplus(-w_lin) - 0.5
        ld_out[:, cols] = -jnp.exp(w_log)
        k = _dot(xk, wk_ref[:, cols])
        v_out[:, cols] = _dot(xv, wv_ref[:, cols]).astype(BF16)
        a = jax.nn.sigmoid(a0_ref[:, cols] + _dot(a_hid, a2_ref[:, cols]))
        a_out[:, cols] = a.astype(BF16)
        g_out[:, cols] = _dot(g_hid, g2_ref[:, cols]).astype(BF16)
        kk_out[:, cols] = (k * kk_ref[:, cols]).astype(BF16)
        k_out[:, cols] = (k * (1.0 + (a - 1.0) * ka_ref[:, cols])).astype(BF16)


def _rwkv_proj(x2, g, mu, wr, wk, wv, w0, w1, w2, a0, a1, a2, g1, g2, k_k, k_a, seq, tm):
    rows, d = x2.shape
    tiles_per_seq = seq // tm
    halo_blocks = tm // HALO
    row_spec = pl.BlockSpec((tm, d), lambda i: (i, 0))
    halo_spec = pl.BlockSpec((HALO, d), lambda i: (jnp.maximum(i * halo_blocks - 1, 0), 0))
    vec = _const_spec((1, d))
    consts = [mu, wr, wk, wv, w0, w1, w2, a0, a1, a2, g1, g2, k_k, k_a]
    return pl.pallas_call(
        functools.partial(_rwkv_proj_kernel, tiles_per_seq=tiles_per_seq, tn=256),
        grid=(rows // tm,),
        in_specs=[row_spec, halo_spec, vec] + [_const_spec(c.shape) for c in consts],
        out_specs=[row_spec] * 7,
        out_shape=[jax.ShapeDtypeStruct((rows, d), F32 if i == 1 else BF16) for i in range(7)],
        compiler_params=_params(("parallel",)),
        name="rwkv_proj",
    )(x2, x2, g, *consts)


def _rwkv_rec_kernel(r_ref, ld_ref, k_ref, v_ref, kk_ref, a_ref, g_ref, rk_ref, lw_ref, lb_ref,
                     o_ref, state):
    L = RWKV_L
    chunks = r_ref.shape[0] // L
    pairs = r_ref.shape[1] // GROUP

    @pl.when(pl.program_id(2) == 0)
    def _():
        state[...] = jnp.zeros(state.shape, F32)

    units = chunks * pairs

    def stack(t):
        return jnp.stack([t[c * L:(c + 1) * L, p * GROUP:(p + 1) * GROUP]
                          for c in range(chunks) for p in range(pairs)], axis=0)

    def per_pair(ref):
        return jnp.stack([ref[:, p * GROUP:(p + 1) * GROUP] for p in range(pairs)], axis=0)

    head0 = lax.broadcasted_iota(jnp.int32, (1, L, GROUP), 2) < HEAD

    def split(t):
        return jnp.concatenate([jnp.where(head0, t, 0.0), jnp.where(head0, 0.0, t)], axis=1)

    ones_r = lax.broadcasted_iota(jnp.int32, (GROUP, GROUP), 0) // HEAD
    ones_c = lax.broadcasted_iota(jnp.int32, (GROUP, GROUP), 1) // HEAD
    head_ones = (ones_r == ones_c).astype(BF16)

    def head_sum(t):
        n = t.shape[0]
        return _dot(t.reshape(n * L, GROUP).astype(BF16), head_ones).reshape(n, L, GROUP)

    row = lax.broadcasted_iota(jnp.int32, (1, 4 * L, 2 * L), 1)
    col = lax.broadcasted_iota(jnp.int32, (1, 4 * L, 2 * L), 2)
    same_head = ((row // L) % 2) == (col // L)
    strict = (col % L) < (row % L)
    keep = same_head & (strict | ((row >= 2 * L) & ((col % L) == (row % L))))
    r2 = lax.broadcasted_iota(jnp.int32, (1, 2 * L, 2 * L), 1)
    c2 = lax.broadcasted_iota(jnp.int32, (1, 2 * L, 2 * L), 2)
    eye = (r2 == c2).astype(F32)
    block_diag = (r2 // HEAD) == (c2 // HEAD)
    tl = lax.broadcasted_iota(jnp.int32, (chunks * L, chunks * L), 0)
    tc = lax.broadcasted_iota(jnp.int32, (chunks * L, chunks * L), 1)
    tri = ((tc <= tl) & ((tc // L) == (tl // L))).astype(BF16)

    lnx_w, lnx_b = per_pair(lw_ref), per_pair(lb_ref)
    r_k = jnp.concatenate([per_pair(rk_ref)] * chunks, axis=0)

    def load(ref):
        return stack(ref[...]).astype(F32)

    ld2 = ld_ref[...]
    width = ld2.shape[1]
    ld_hi = ld2.astype(BF16)
    rem = ld2 - ld_hi.astype(F32)
    ld_mid = rem.astype(BF16)
    ld_lo = (rem - ld_mid.astype(F32)).astype(BF16)
    c3 = _dot(tri, jnp.concatenate([ld_hi, ld_mid, ld_lo], axis=1))
    cum = stack(c3[:, :width] + c3[:, width:2 * width] + c3[:, 2 * width:])

    r, ld, kx, v, kk, a = load(r_ref), stack(ld2), load(k_ref), load(v_ref), load(kk_ref), load(a_ref)
    kkn = kk * jnp.minimum(lax.rsqrt(head_sum(kk * kk)), 1e12)
    p_in = jnp.exp(cum)
    inv_p = jnp.exp(-cum)
    rt = r * p_in
    at = -kkn * jnp.exp(cum - ld)
    bt = (kkn * a * inv_p).astype(BF16)
    kt = (kx * inv_p).astype(BF16)

    lhs = jnp.concatenate([at, rt], axis=1).astype(BF16)
    xbk = _bdot_nt(lhs, jnp.concatenate([split(bt), split(kt)], axis=1))
    xa, xr = xbk[:, :L], xbk[:, L:]
    xb = jnp.where(keep, jnp.concatenate([xa[:, :, :2 * L]] * 2 + [xr[:, :, :2 * L]] * 2, axis=1), 0.0)
    xk = jnp.where(keep, jnp.concatenate([xa[:, :, 2 * L:]] * 2 + [xr[:, :, 2 * L:]] * 2, axis=1), 0.0)
    m_ab = xb[:, :2 * L]
    m_r = jnp.concatenate([xb[:, 2 * L:], xk[:, 2 * L:]], axis=2).astype(BF16)
    bk = jnp.concatenate([bt, kt], axis=1)
    v_st = split(v).astype(BF16)
    p_last = p_in[:, L - 1:L, :]
    bonus = head_sum(r * kx * r_k) * v
    gate = load(g_ref)

    n = _bdot(m_ab.astype(BF16), m_ab.astype(BF16))
    t_inv = eye + m_ab
    levels = int(math.log2(L))
    for level in range(1, levels):
        nb = n.astype(BF16)
        if level < levels - 1:
            prod = _bdot(nb, jnp.concatenate([nb, t_inv.astype(BF16)], axis=2))
            n, t_inv = prod[:, :, :2 * L], t_inv + prod[:, :, 2 * L:]
        else:
            t_inv = t_inv + _bdot(nb, t_inv.astype(BF16))
    t_inv = t_inv.astype(BF16)
    akv = _bdot(xk[:, :2 * L].astype(BF16), v_st)

    s_bd = state[...]
    for c in range(chunks):
        u = slice(c * pairs, (c + 1) * pairs)
        xs = _bdot_nt(lhs[u], s_bd.astype(BF16))
        sa_st = _bdot(t_inv[u], (split(xs[:, :L]) + akv[u]).astype(BF16))
        y_st = _bdot(m_r[u], jnp.concatenate([sa_st.astype(BF16), v_st[u]], axis=1))
        y = xs[:, L:] + y_st[:, :L] + y_st[:, L:]
        sa = sa_st[:, :L] + sa_st[:, L:]
        ds = _bdot_tn(jnp.concatenate([sa, v[u]], axis=1).astype(BF16), bk[u])
        s_bd = jnp.where(block_diag, s_bd + ds, 0.0) * p_last[u]

        mean = head_sum(y) * (1.0 / HEAD)
        dev = y - mean
        var = head_sum(dev * dev) * (1.0 / HEAD)
        yn = dev * lax.rsqrt(var + RWKV_GN_EPS) * lnx_w + lnx_b
        out = ((yn + bonus[u]) * gate[u]).astype(BF16)
        for p in range(pairs):
            o_ref[c * L:(c + 1) * L, p * GROUP:(p + 1) * GROUP] = out[p]
    state[...] = s_bd


def _rwkv_rec(r, ld, k, v, kk, a, g, r_k, lnx_w, lnx_b, batch, seq, pairs, chunks):
    rows, d = r.shape
    width = pairs * GROUP
    groups = d // width
    nc = seq // (chunks * RWKV_L)
    spec = pl.BlockSpec((chunks * RWKV_L, width), lambda b, p, c: (b * nc + c, p))
    vec = pl.BlockSpec((1, width), lambda b, p, c: (0, p))
    return pl.pallas_call(
        _rwkv_rec_kernel,
        grid=(batch, groups, nc),
        in_specs=[spec] * 7 + [vec] * 3,
        out_specs=spec,
        out_shape=jax.ShapeDtypeStruct((rows, d), BF16),
        scratch_shapes=[pltpu.VMEM((pairs, GROUP, GROUP), F32)],
        compiler_params=_params(("parallel", "parallel", "arbitrary")),
        name="rwkv_recurrence",
    )(r, ld, k, v, kk, a, g, r_k, lnx_w, lnx_b)


def _rotary_tables(seq):
    inv_freq = 1.0 / (ROPE_THETA ** (jnp.arange(0, HEAD, 2, dtype=F32) / HEAD))
    ang = jnp.arange(seq, dtype=F32)[:, None] * inv_freq[None, :]
    cos, sin = jnp.cos(ang), jnp.sin(ang)
    cos = jnp.concatenate([cos, cos, cos, cos], axis=1)
    sin = jnp.concatenate([-sin, sin, -sin, sin], axis=1)
    return cos, sin


def _pad_cols(w):
    return jnp.pad(w, ((0, 0), (0, LORA_PAD - w.shape[1])))


def _pad_rows(w):
    return jnp.pad(w, ((0, LORA_PAD - w.shape[0]), (0, 0)))


def kernel(x, e_norm_g, e_w_in, e_pool_w, e_pool_scale, e_lambda, e_subln_g, e_w_o, o_norm_g, o_mu, o_w_r, o_w_k, o_w_v, o_w_o, o_w0, o_w1, o_w2, o_a0, o_a1, o_a2, o_g1, o_g2, o_k_k, o_k_a, o_r_k, o_lnx_w, o_lnx_b, f_norm_g, f_w_in, f_conv_w, f_conv_b, f_w_out, final_g):
    batch, seq, d = x.shape
    depth = f_norm_g.shape[0]
    tm = min(512, seq)
    attn_t = min(512, seq)
    x2 = x.reshape(batch * seq, d)
    row = lambda vec: vec.reshape(1, -1)
    cos, sin = _rotary_tables(seq)

    for layer in range(depth):
        i = layer // 2
        if layer % 2 == 0:
            lambda_init = 0.8 - 0.6 * math.exp(-0.3 * layer)
            u, q, k, v = _even_proj(x2, row(e_norm_g[i]), e_w_in[i].astype(BF16), cos, sin, seq, tm)
            pool_out = _pool(u, e_pool_w[i].astype(BF16), row(e_pool_scale[i]), batch, seq, tm)
            attn_out = _diff_attention(q, k, v, e_lambda[i], row(e_subln_g[i]), batch, seq,
                                       lambda_init, attn_t)
            x2 = _proj_res(x2, [pool_out, attn_out], e_w_o[i].astype(BF16), tm)
        else:
            r, ld, k, v, kk, a, g = _rwkv_proj(
                x2, row(o_norm_g[i]), o_mu[i], o_w_r[i].astype(BF16), o_w_k[i].astype(BF16),
                o_w_v[i].astype(BF16), row(o_w0[i]), _pad_cols(o_w1[i]).astype(BF16),
                _pad_rows(o_w2[i]).astype(BF16), row(o_a0[i]), _pad_cols(o_a1[i]).astype(BF16),
                _pad_rows(o_a2[i]).astype(BF16), o_g1[i].astype(BF16), o_g2[i].astype(BF16),
                row(o_k_k[i]), row(o_k_a[i]), seq, tm)
            z = _rwkv_rec(r, ld, k, v, kk, a, g, row(o_r_k[i]), row(o_lnx_w[i]), row(o_lnx_b[i]),
                          batch, seq, pairs=8, chunks=4)
            x2 = _proj_res(x2, [z], o_w_o[i].astype(BF16), tm)
        x2 = _ffn(x2, row(f_norm_g[layer]), f_w_in[layer].astype(BF16), f_conv_w[layer],
                  row(f_conv_b[layer]), f_w_out[layer].astype(BF16), row(final_g), seq,
                  min(1024, seq), tf=256, final_norm=(layer == depth - 1))
    return x2.reshape(batch, seq, d)
```

```python
import functools
import math

import jax
import jax.numpy as jnp
from jax import lax
from jax.experimental import pallas as pl
from jax.experimental.pallas import tpu as pltpu

F32 = jnp.float32
BF16 = jnp.bfloat16

RMS_EPS = 1e-5
ROPE_THETA = 10000.0
CHUNK = 64
POOL_WINDOWS = (2, 4, 8, 16)
POOL_HALO = 16
GROUP = 128
HEAD = 64
RWKV_GN_EPS = 64e-5
RWKV_L = 64
LORA_PAD = 128
HALO = 8
NEG = -1e30
LOG2_E = 1.4426950408889634
VMEM_LIMIT = 56 * 1024 * 1024


def _rms(x, g):
    return x * lax.rsqrt(jnp.mean(x * x, axis=-1, keepdims=True) + RMS_EPS) * g


def _dot(a, b):
    return jnp.dot(a, b, preferred_element_type=F32)


def _dot_nt(a, b):
    return lax.dot_general(a, b, (((1,), (1,)), ((), ())), preferred_element_type=F32)


def _dot_tn(a, b):
    return lax.dot_general(a, b, (((0,), (0,)), ((), ())), preferred_element_type=F32)


def _bdot(a, b):
    return lax.dot_general(a, b, (((2,), (1,)), ((0,), (0,))), preferred_element_type=F32)


def _bdot_nt(a, b):
    return lax.dot_general(a, b, (((2,), (2,)), ((0,), (0,))), preferred_element_type=F32)


def _bdot_tn(a, b):
    return lax.dot_general(a, b, (((1,), (1,)), ((0,), (0,))), preferred_element_type=F32)


def _shift_rows(t, halo, n):
    rolled = pltpu.roll(t, n, 0)
    row = lax.broadcasted_iota(jnp.int32, halo.shape, 0)
    head = jnp.where(row < n, pltpu.roll(halo, n, 0), rolled[:HALO])
    return jnp.concatenate([head, rolled[HALO:]], axis=0)


def _const_spec(shape):
    return pl.BlockSpec(shape, lambda *_: (0,) * len(shape), pipeline_mode=pl.Buffered(1))


def _params(sem):
    return pltpu.CompilerParams(dimension_semantics=sem, vmem_limit_bytes=VMEM_LIMIT)


def _even_proj_kernel(x_ref, halo_ref, g_ref, w_ref, cos_ref, sin_ref, pw_ref, ps_ref,
                      pool_ref, q_ref, k_ref, v_ref, *, tiles_per_seq):
    tm, width = q_ref.shape
    g = g_ref[...]
    xn = _rms(x_ref[...], g).astype(BF16)
    cos = jnp.concatenate([cos_ref[...]] * (width // GROUP), axis=1)
    sin = jnp.concatenate([sin_ref[...]] * (width // GROUP), axis=1)
    lane = lax.broadcasted_iota(jnp.int32, cos.shape, 1)
    first_half = (lane % HEAD) < HEAD // 2

    def proj(idx):
        return _dot(xn, w_ref[:, idx * width:(idx + 1) * width])

    def rope(t):
        partner = jnp.where(first_half, pltpu.roll(t, width - HEAD // 2, 1),
                            pltpu.roll(t, HEAD // 2, 1))
        return t * cos + partner * sin

    q_ref[...] = (rope(proj(1)) * (HEAD ** -0.5 * LOG2_E)).astype(BF16)
    k_ref[...] = rope(proj(2)).astype(BF16)
    v_ref[...] = proj(3).astype(BF16)

    tile_idx = pl.program_id(0) % tiles_per_seq
    u = proj(0)
    xh = jnp.where(tile_idx == 0, 0.0, _rms(halo_ref[...], g)).astype(BF16)
    ext = jnp.concatenate([_dot(xh, w_ref[:, :width]), u], axis=0)
    pos = tile_idx * tm + lax.broadcasted_iota(jnp.int32, (tm, 1), 0)
    for gi, win in enumerate(POOL_WINDOWS):
        sl = slice(gi * GROUP, (gi + 1) * GROUP)
        acc, span = ext[:, sl], 1
        while span < win:
            acc = acc + pltpu.roll(acc, span, 0)
            span *= 2
        inv_count = 1.0 / jnp.minimum(pos + 1, win).astype(F32)
        delta = acc[POOL_HALO:] * inv_count - u[:, sl]
        y = _dot(delta.astype(BF16), pw_ref[gi]) * ps_ref[:, sl]
        pool_ref[:, sl] = y.astype(BF16)


def _even_proj(x2, g, w, cos, sin, pool_w, pool_scale, seq, tm):
    rows, d = x2.shape
    width = w.shape[1] // 4
    tiles_per_seq = seq // tm
    halo_blocks = tm // POOL_HALO
    row_spec = pl.BlockSpec((tm, d), lambda i: (i, 0))
    halo_spec = pl.BlockSpec((POOL_HALO, d), lambda i: (jnp.maximum(i * halo_blocks - 1, 0), 0))
    out_spec = pl.BlockSpec((tm, width), lambda i: (i, 0))
    tab_spec = pl.BlockSpec((tm, GROUP), lambda i: (i % tiles_per_seq, 0))
    return pl.pallas_call(
        functools.partial(_even_proj_kernel, tiles_per_seq=tiles_per_seq),
        grid=(rows // tm,),
        in_specs=[row_spec, halo_spec, _const_spec((1, d)), _const_spec(w.shape), tab_spec, tab_spec,
                  _const_spec(pool_w.shape), _const_spec((1, width))],
        out_specs=[out_spec] * 4,
        out_shape=[jax.ShapeDtypeStruct((rows, width), BF16)] * 4,
        compiler_params=_params(("parallel",)),
        name="even_proj_pool",
    )(x2, x2, g, w, cos, sin, pool_w, pool_scale)


def _attn_kernel(lam_ref, q_ref, k_ref, v_ref, sg_ref, o_ref, s_buf, *, lambda_init):
    t = q_ref.shape[0]
    i = pl.program_id(2)
    q = q_ref[...]
    lane = lax.broadcasted_iota(jnp.int32, q.shape, 1)
    zero = jnp.zeros_like(q)
    qs = jnp.concatenate([jnp.where(lane < HEAD, q, zero), jnp.where(lane < HEAD, zero, q)], axis=0)

    def lane_fold(op, run, s):
        for c in range(t // GROUP):
            run = op(run, s[:, c * GROUP:(c + 1) * GROUP])
        return run

    def tile(nblk):
        run = jnp.full((2 * t, GROUP), NEG, F32)
        for j in range(nblk):
            kb = k_ref[j * t:(j + 1) * t, :]
            s = jnp.concatenate([_dot_nt(qs[:t], kb), _dot_nt(qs[t:], kb)], axis=0)
            if j == nblk - 1:
                row = lax.broadcasted_iota(jnp.int32, s.shape, 0)
                col = lax.broadcasted_iota(jnp.int32, s.shape, 1)
                s = jnp.where((col // CHUNK) <= ((row % t) // CHUNK), s, NEG)
            s_buf[j] = s
            run = lane_fold(jnp.maximum, run, s)
        m = jnp.max(run, axis=-1, keepdims=True)

        acc = jnp.zeros((2 * t, 2 * GROUP), F32)
        ones = jnp.ones((t, GROUP), BF16)
        for j in range(nblk):
            pb = jnp.exp2((s_buf[j] - m).astype(BF16))
            v1 = jnp.concatenate([v_ref[j * t:(j + 1) * t, :], ones], axis=1)
            acc = acc + jnp.concatenate([_dot(pb[:t], v1), _dot(pb[t:], v1)], axis=0)
        o = acc[:, :GROUP] / acc[:, GROUP:]
        lv = lam_ref[...]
        lam = (jnp.exp(jnp.sum(lv[0:1] * lv[1:2], axis=-1, keepdims=True))
               - jnp.exp(jnp.sum(lv[2:3] * lv[3:4], axis=-1, keepdims=True)) + lambda_init)
        o = o[:t] - lam * o[t:]
        o = _rms(o, sg_ref[...]) * (1.0 - lambda_init)
        o_ref[...] = o.astype(BF16)

    for c in range(s_buf.shape[0]):
        pl.when(i == c)(functools.partial(tile, c + 1))


def _diff_attention(q, k, v, lam_vecs, subln_g, batch, seq, lambda_init, t):
    rows, width = q.shape
    heads = width // GROUP
    nq = seq // t
    q_spec = pl.BlockSpec((t, GROUP), lambda b, h, i: (b * nq + i, h))
    kv_spec = pl.BlockSpec((seq, GROUP), lambda b, h, i: (b, h))
    return pl.pallas_call(
        functools.partial(_attn_kernel, lambda_init=lambda_init),
        grid=(batch, heads, nq),
        in_specs=[_const_spec(lam_vecs.shape), q_spec, kv_spec, kv_spec, _const_spec((1, GROUP))],
        out_specs=q_spec,
        out_shape=jax.ShapeDtypeStruct((rows, width), BF16),
        scratch_shapes=[pltpu.VMEM((nq, 2 * t, t), F32)],
        compiler_params=_params(("parallel", "parallel", "arbitrary")),
        name="diff_attention",
    )(lam_vecs, q, k, v, subln_g)


def _ffn_kernel(*refs, n_acts, tiles_per_seq, tf, final_norm):
    x_ref, halo_ref = refs[0], refs[1]
    act_refs, act_halo_refs = refs[2:2 + n_acts], refs[2 + n_acts:2 + 2 * n_acts]
    wm_ref, g_ref, wi_ref, cw_ref, cb_ref, wo_ref, fg_ref, o_ref, abuf, hbuf = refs[2 + 2 * n_acts:]
    tm = x_ref.shape[0]
    d_ff = hbuf.shape[1]

    def mixed(res, acts):
        row0 = 0
        for a in acts:
            res = res + _dot(a, wm_ref[row0:row0 + a.shape[1], :])[-res.shape[0]:]
            row0 += a.shape[1]
        return res

    x = mixed(x_ref[...], [a[...] for a in act_refs])
    g = g_ref[...]
    xn = _rms(x, g).astype(BF16)
    seq_start = (pl.program_id(0) % tiles_per_seq) == 0
    x_halo = mixed(halo_ref[...], [a[...] for a in act_halo_refs])
    xh = jnp.where(seq_start, 0.0, _rms(x_halo, g)).astype(BF16)

    for c in range(d_ff // tf):
        cols = slice(c * tf, (c + 1) * tf)
        slot = c % 2
        wa = wi_ref[:, cols]
        abuf[slot, 0:HALO, :] = _dot(xh, wa)
        a = _dot(xn, wa)
        abuf[slot, HALO:, :] = a
        gate = _dot(xn, wi_ref[:, d_ff + c * tf:d_ff + (c + 1) * tf])
        cw = cw_ref[:, cols]
        conv = (abuf[slot, HALO - 2:HALO - 2 + tm, :] * cw[0:1]
                + abuf[slot, HALO - 1:HALO - 1 + tm, :] * cw[1:2] + a * cw[2:3] + cb_ref[:, cols])
        act = 0.5 * conv * (1.0 + lax.erf(conv * (2.0 ** -0.5)))
        hbuf[:, cols] = (act * gate).astype(BF16)
    out = x + _dot(hbuf[...], wo_ref[...])
    if final_norm:
        out = _rms(out, fg_ref[...])
    o_ref[...] = out


def _ffn(x2, acts, w_mix, g, w_in, conv_w, conv_b, w_out, final_g, seq, tm, tf, final_norm):
    rows, d = x2.shape
    d_ff = w_out.shape[0]
    tiles_per_seq = seq // tm
    row_spec = pl.BlockSpec((tm, d), lambda i: (i, 0))
    halo_spec = pl.BlockSpec((HALO, d), lambda i: (jnp.maximum(i * (tm // HALO) - 1, 0), 0))
    act_specs = [pl.BlockSpec((tm, a.shape[1]), lambda i: (i, 0)) for a in acts]
    act_halo_specs = [pl.BlockSpec((2 * HALO, a.shape[1]),
                                   lambda i: (jnp.maximum(i * (tm // (2 * HALO)) - 1, 0), 0)) for a in acts]
    return pl.pallas_call(
        functools.partial(_ffn_kernel, n_acts=len(acts), tiles_per_seq=tiles_per_seq, tf=tf,
                          final_norm=final_norm),
        grid=(rows // tm,),
        in_specs=[row_spec, halo_spec] + act_specs + act_halo_specs
        + [_const_spec(w_mix.shape), _const_spec((1, d)), _const_spec(w_in.shape),
           _const_spec(conv_w.shape), _const_spec((1, d_ff)), _const_spec(w_out.shape),
           _const_spec((1, d))],
        out_specs=row_spec,
        out_shape=jax.ShapeDtypeStruct((rows, d), F32),
        scratch_shapes=[pltpu.VMEM((2, HALO + tm, tf), F32), pltpu.VMEM((tm, d_ff), BF16)],
        compiler_params=_params(("parallel",)),
        name="mix_proj_conv_ffn",
    )(x2, x2, *acts, *acts, w_mix, g, w_in, conv_w, conv_b, w_out, final_g)


def _rwkv_proj_kernel(x_ref, halo_ref, g_ref, mu_ref, wr_ref, wk_ref, wv_ref, w0_ref, w1_ref,
                      w2_ref, a0_ref, a1_ref, a2_ref, g1_ref, g2_ref, kk_ref, ka_ref,
                      r_out, ld_out, k_out, v_out, kk_out, a_out, g_out, *, tiles_per_seq, tn):
    g = g_ref[...]
    h = _rms(x_ref[...], g)
    seq_start = (pl.program_id(0) % tiles_per_seq) == 0
    h_halo = jnp.where(seq_start, 0.0, _rms(halo_ref[...], g))
    xx = _shift_rows(h, h_halo, 1) - h
    hb, xb, mu = h.astype(BF16), xx.astype(BF16), mu_ref[...].astype(BF16)

    def mix(idx):
        return hb + xb * mu[idx:idx + 1, :]

    xr, xk, xv = mix(0), mix(2), mix(3)
    w_hid = jnp.tanh(_dot(mix(1), w1_ref[...])).astype(BF16)
    a_hid = _dot(mix(4), a1_ref[...]).astype(BF16)
    g_hid = jax.nn.sigmoid(_dot(mix(5), g1_ref[...])).astype(BF16)
    for c in range(h.shape[1] // tn):
        cols = slice(c * tn, (c + 1) * tn)
        r_out[:, cols] = _dot(xr, wr_ref[:, cols]).astype(BF16)
        w_lin = w0_ref[:, cols] + _dot(w_hid, w2_ref[:, cols])
        w_log = -jax.nn.softplus(-w_lin) - 0.5
        ld_out[:, cols] = -jnp.exp(w_log)
        k = _dot(xk, wk_ref[:, cols])
        v_out[:, cols] = _dot(xv, wv_ref[:, cols]).astype(BF16)
        a = jax.nn.sigmoid(a0_ref[:, cols] + _dot(a_hid, a2_ref[:, cols]))
        a_out[:, cols] = a.astype(BF16)
        g_out[:, cols] = _dot(g_hid, g2_ref[:, cols]).astype(BF16)
        kk_out[:, cols] = (k * kk_ref[:, cols]).astype(BF16)
        k_out[:, cols] = (k * (1.0 + (a - 1.0) * ka_ref[:, cols])).astype(BF16)


def _rwkv_proj(x2, g, mu, wr, wk, wv, w0, w1, w2, a0, a1, a2, g1, g2, k_k, k_a, seq, tm):
    rows, d = x2.shape
    tiles_per_seq = seq // tm
    halo_blocks = tm // HALO
    row_spec = pl.BlockSpec((tm, d), lambda i: (i, 0))
    halo_spec = pl.BlockSpec((HALO, d), lambda i: (jnp.maximum(i * halo_blocks - 1, 0), 0))
    vec = _const_spec((1, d))
    consts = [mu, wr, wk, wv, w0, w1, w2, a0, a1, a2, g1, g2, k_k, k_a]
    return pl.pallas_call(
        functools.partial(_rwkv_proj_kernel, tiles_per_seq=tiles_per_seq, tn=256),
        grid=(rows // tm,),
        in_specs=[row_spec, halo_spec, vec] + [_const_spec(c.shape) for c in consts],
        out_specs=[row_spec] * 7,
        out_shape=[jax.ShapeDtypeStruct((rows, d), F32 if i == 1 else BF16) for i in range(7)],
        compiler_params=_params(("parallel",)),
        name="rwkv_proj",
    )(x2, x2, g, *consts)


def _rwkv_rec_kernel(r_ref, ld_ref, k_ref, v_ref, kk_ref, a_ref, g_ref, rk_ref, lw_ref, lb_ref,
                     o_ref, state):
    L = RWKV_L
    chunks = r_ref.shape[0] // L
    pairs = r_ref.shape[1] // GROUP

    @pl.when(pl.program_id(2) == 0)
    def _():
        state[...] = jnp.zeros(state.shape, F32)

    units = chunks * pairs

    def stack(t):
        return jnp.stack([t[c * L:(c + 1) * L, p * GROUP:(p + 1) * GROUP]
                          for c in range(chunks) for p in range(pairs)], axis=0)

    def per_pair(ref):
        return jnp.stack([ref[:, p * GROUP:(p + 1) * GROUP] for p in range(pairs)], axis=0)

    head0 = lax.broadcasted_iota(jnp.int32, (1, L, GROUP), 2) < HEAD

    def split(t):
        return jnp.concatenate([jnp.where(head0, t, 0.0), jnp.where(head0, 0.0, t)], axis=1)

    ones_r = lax.broadcasted_iota(jnp.int32, (GROUP, GROUP), 0) // HEAD
    ones_c = lax.broadcasted_iota(jnp.int32, (GROUP, GROUP), 1) // HEAD
    head_ones = (ones_r == ones_c).astype(BF16)

    def head_sum(t):
        n = t.shape[0]
        return _dot(t.reshape(n * L, GROUP).astype(BF16), head_ones).reshape(n, L, GROUP)

    row = lax.broadcasted_iota(jnp.int32, (1, 4 * L, 2 * L), 1)
    col = lax.broadcasted_iota(jnp.int32, (1, 4 * L, 2 * L), 2)
    same_head = ((row // L) % 2) == (col // L)
    strict = (col % L) < (row % L)
    keep = same_head & (strict | ((row >= 2 * L) & ((col % L) == (row % L))))
    r2 = lax.broadcasted_iota(jnp.int32, (1, 2 * L, 2 * L), 1)
    c2 = lax.broadcasted_iota(jnp.int32, (1, 2 * L, 2 * L), 2)
    eye = (r2 == c2).astype(F32)
    block_diag = (r2 // HEAD) == (c2 // HEAD)
    tl = lax.broadcasted_iota(jnp.int32, (chunks * L, chunks * L), 0)
    tc = lax.broadcasted_iota(jnp.int32, (chunks * L, chunks * L), 1)
    tri = ((tc <= tl) & ((tc // L) == (tl // L))).astype(BF16)

    lnx_w, lnx_b = per_pair(lw_ref), per_pair(lb_ref)
    r_k = jnp.concatenate([per_pair(rk_ref)] * chunks, axis=0)

    def load(ref):
        return stack(ref[...]).astype(F32)

    ld2 = ld_ref[...]
    width = ld2.shape[1]
    ld_hi = ld2.astype(BF16)
    rem = ld2 - ld_hi.astype(F32)
    ld_mid = rem.astype(BF16)
    ld_lo = (rem - ld_mid.astype(F32)).astype(BF16)
    c3 = _dot(tri, jnp.concatenate([ld_hi, ld_mid, ld_lo], axis=1))
    cum = stack(c3[:, :width] + c3[:, width:2 * width] + c3[:, 2 * width:])

    r, ld, kx, v, kk, a = load(r_ref), stack(ld2), load(k_ref), load(v_ref), load(kk_ref), load(a_ref)
    kkn = kk * jnp.minimum(lax.rsqrt(head_sum(kk * kk)), 1e12)
    p_in = jnp.exp(cum)
    inv_p = jnp.exp(-cum)
    rt = r * p_in
    at = -kkn * jnp.exp(cum - ld)
    bt = (kkn * a * inv_p).astype(BF16)
    kt = (kx * inv_p).astype(BF16)

    lhs = jnp.concatenate([at, rt], axis=1).astype(BF16)
    xbk = _bdot_nt(lhs, jnp.concatenate([split(bt), split(kt)], axis=1))
    xa, xr = xbk[:, :L], xbk[:, L:]
    xb = jnp.where(keep, jnp.concatenate([xa[:, :, :2 * L]] * 2 + [xr[:, :, :2 * L]] * 2, axis=1), 0.0)
    xk = jnp.where(keep, jnp.concatenate([xa[:, :, 2 * L:]] * 2 + [xr[:, :, 2 * L:]] * 2, axis=1), 0.0)
    m_ab = xb[:, :2 * L]
    m_r = jnp.concatenate([xb[:, 2 * L:], xk[:, 2 * L:]], axis=2).astype(BF16)
    bk = jnp.concatenate([bt, kt], axis=1)
    v_st = split(v).astype(BF16)
    p_last = p_in[:, L - 1:L, :]
    bonus = head_sum(r * kx * r_k) * v
    gate = load(g_ref)

    n = _bdot(m_ab.astype(BF16), m_ab.astype(BF16))
    t_inv = eye + m_ab
    levels = int(math.log2(L))
    for level in range(1, levels):
        nb = n.astype(BF16)
        if level < levels - 1:
            prod = _bdot(nb, jnp.concatenate([nb, t_inv.astype(BF16)], axis=2))
            n, t_inv = prod[:, :, :2 * L], t_inv + prod[:, :, 2 * L:]
        else:
            t_inv = t_inv + _bdot(nb, t_inv.astype(BF16))
    t_inv = t_inv.astype(BF16)
    akv = _bdot(xk[:, :2 * L].astype(BF16), v_st)

    s_bd = state[...]
    for c in range(chunks):
        u = slice(c * pairs, (c + 1) * pairs)
        xs = _bdot_nt(lhs[u], s_bd.astype(BF16))
        sa_st = _bdot(t_inv[u], (split(xs[:, :L]) + akv[u]).astype(BF16))
        y_st = _bdot(m_r[u], jnp.concatenate([sa_st.astype(BF16), v_st[u]], axis=1))
        y = xs[:, L:] + y_st[:, :L] + y_st[:, L:]
        sa = sa_st[:, :L] + sa_st[:, L:]
        ds = _bdot_tn(jnp.concatenate([sa, v[u]], axis=1).astype(BF16), bk[u])
        s_bd = jnp.where(block_diag, s_bd + ds, 0.0) * p_last[u]

        mean = head_sum(y) * (1.0 / HEAD)
        dev = y - mean
        var = head_sum(dev * dev) * (1.0 / HEAD)
        yn = dev * lax.rsqrt(var + RWKV_GN_EPS) * lnx_w + lnx_b
        out = ((yn + bonus[u]) * gate[u]).astype(BF16)
        for p in range(pairs):
            o_ref[c * L:(c + 1) * L, p * GROUP:(p + 1) * GROUP] = out[p]
    state[...] = s_bd


def _rwkv_rec(r, ld, k, v, kk, a, g, r_k, lnx_w, lnx_b, batch, seq, pairs, chunks):
    rows, d = r.shape
    width = pairs * GROUP
    groups = d // width
    nc = seq // (chunks * RWKV_L)
    spec = pl.BlockSpec((chunks * RWKV_L, width), lambda b, p, c: (b * nc + c, p))
    vec = pl.BlockSpec((1, width), lambda b, p, c: (0, p))
    return pl.pallas_call(
        _rwkv_rec_kernel,
        grid=(batch, groups, nc),
        in_specs=[spec] * 7 + [vec] * 3,
        out_specs=spec,
        out_shape=jax.ShapeDtypeStruct((rows, d), BF16),
        scratch_shapes=[pltpu.VMEM((pairs, GROUP, GROUP), F32)],
        compiler_params=_params(("parallel", "parallel", "arbitrary")),
        name="rwkv_recurrence",
    )(r, ld, k, v, kk, a, g, r_k, lnx_w, lnx_b)


def _rotary_tables(seq):
    inv_freq = 1.0 / (ROPE_THETA ** (jnp.arange(0, HEAD, 2, dtype=F32) / HEAD))
    ang = jnp.arange(seq, dtype=F32)[:, None] * inv_freq[None, :]
    cos, sin = jnp.cos(ang), jnp.sin(ang)
    cos = jnp.concatenate([cos, cos, cos, cos], axis=1)
    sin = jnp.concatenate([-sin, sin, -sin, sin], axis=1)
    return cos, sin


def _pad_cols(w):
    return jnp.pad(w, ((0, 0), (0, LORA_PAD - w.shape[1])))


def _pad_rows(w):
    return jnp.pad(w, ((0, LORA_PAD - w.shape[0]), (0, 0)))


def kernel(x, e_norm_g, e_w_in, e_pool_w, e_pool_scale, e_lambda, e_subln_g, e_w_o, o_norm_g, o_mu, o_w_r, o_w_k, o_w_v, o_w_o, o_w0, o_w1, o_w2, o_a0, o_a1, o_a2, o_g1, o_g2, o_k_k, o_k_a, o_r_k, o_lnx_w, o_lnx_b, f_norm_g, f_w_in, f_conv_w, f_conv_b, f_w_out, final_g):
    batch, seq, d = x.shape
    depth = f_norm_g.shape[0]
    tm = min(512, seq)
    attn_t = min(512, seq)
    x2 = x.reshape(batch * seq, d)
    row = lambda vec: vec.reshape(1, -1)
    cos, sin = _rotary_tables(seq)

    for layer in range(depth):
        i = layer // 2
        if layer % 2 == 0:
            lambda_init = 0.8 - 0.6 * math.exp(-0.3 * layer)
            pool_out, q, k, v = _even_proj(x2, row(e_norm_g[i]), e_w_in[i].astype(BF16), cos, sin,
                                           e_pool_w[i].astype(BF16), row(e_pool_scale[i]), seq, tm)
            attn_out = _diff_attention(q, k, v, e_lambda[i], row(e_subln_g[i]), batch, seq,
                                       lambda_init, attn_t)
            acts, w_mix = [pool_out, attn_out], e_w_o[i].astype(BF16)
        else:
            r, ld, k, v, kk, a, g = _rwkv_proj(
                x2, row(o_norm_g[i]), o_mu[i], o_w_r[i].astype(BF16), o_w_k[i].astype(BF16),
                o_w_v[i].astype(BF16), row(o_w0[i]), _pad_cols(o_w1[i]).astype(BF16),
                _pad_rows(o_w2[i]).astype(BF16), row(o_a0[i]), _pad_cols(o_a1[i]).astype(BF16),
                _pad_rows(o_a2[i]).astype(BF16), o_g1[i].astype(BF16), o_g2[i].astype(BF16),
                row(o_k_k[i]), row(o_k_a[i]), seq, tm)
            z = _rwkv_rec(r, ld, k, v, kk, a, g, row(o_r_k[i]), row(o_lnx_w[i]), row(o_lnx_b[i]),
                          batch, seq, pairs=8, chunks=4)
            acts, w_mix = [z], o_w_o[i].astype(BF16)
        x2 = _ffn(x2, acts, w_mix, row(f_norm_g[layer]), f_w_in[layer].astype(BF16), f_conv_w[layer],
                  row(f_conv_b[layer]), f_w_out[layer].astype(BF16), row(final_g), seq,
                  min(1024, seq), tf=256, final_norm=(layer == depth - 1))
    return x2.reshape(batch, seq, d)
```

```python
import functools
import math

import jax
import jax.numpy as jnp
from jax import lax
from jax.experimental import pallas as pl
from jax.experimental.pallas import tpu as pltpu

F32 = jnp.float32
BF16 = jnp.bfloat16

RMS_EPS = 1e-5
ROPE_THETA = 10000.0
CHUNK = 64
POOL_WINDOWS = (2, 4, 8, 16)
POOL_HALO = 16
GROUP = 128
HEAD = 64
RWKV_GN_EPS = 64e-5
RWKV_L = 64
LORA_PAD = 128
HALO = 8
NEG = -1e30
LOG2_E = 1.4426950408889634
VMEM_LIMIT = 56 * 1024 * 1024


def _rms(x, g):
    return x * lax.rsqrt(jnp.mean(x * x, axis=-1, keepdims=True) + RMS_EPS) * g


def _dot(a, b):
    return jnp.dot(a, b, preferred_element_type=F32)


def _dot_nt(a, b):
    return lax.dot_general(a, b, (((1,), (1,)), ((), ())), preferred_element_type=F32)


def _dot_tn(a, b):
    return lax.dot_general(a, b, (((0,), (0,)), ((), ())), preferred_element_type=F32)


def _bdot(a, b):
    return lax.dot_general(a, b, (((2,), (1,)), ((0,), (0,))), preferred_element_type=F32)


def _bdot_nt(a, b):
    return lax.dot_general(a, b, (((2,), (2,)), ((0,), (0,))), preferred_element_type=F32)


def _bdot_tn(a, b):
    return lax.dot_general(a, b, (((1,), (1,)), ((0,), (0,))), preferred_element_type=F32)


def _shift_rows(t, halo, n):
    rolled = pltpu.roll(t, n, 0)
    row = lax.broadcasted_iota(jnp.int32, halo.shape, 0)
    head = jnp.where(row < n, pltpu.roll(halo, n, 0), rolled[:HALO])
    return jnp.concatenate([head, rolled[HALO:]], axis=0)


def _const_spec(shape):
    return pl.BlockSpec(shape, lambda *_: (0,) * len(shape), pipeline_mode=pl.Buffered(1))


def _layer_spec(shape, layer):
    return pl.BlockSpec((None,) + tuple(shape[1:]), lambda *_: (layer,) + (0,) * (len(shape) - 1),
                        pipeline_mode=pl.Buffered(1))


def _params(sem):
    return pltpu.CompilerParams(dimension_semantics=sem, vmem_limit_bytes=VMEM_LIMIT)


def _even_proj_kernel(x_ref, halo_ref, g_ref, w_ref, cos_ref, sin_ref, pw_ref, ps_ref,
                      pool_ref, q_ref, k_ref, v_ref, *, tiles_per_seq):
    tm, width = q_ref.shape
    g = g_ref[...]
    xn = _rms(x_ref[...], g).astype(BF16)
    cos = jnp.concatenate([cos_ref[...]] * (width // GROUP), axis=1)
    sin = jnp.concatenate([sin_ref[...]] * (width // GROUP), axis=1)
    lane = lax.broadcasted_iota(jnp.int32, cos.shape, 1)
    first_half = (lane % HEAD) < HEAD // 2

    def proj(idx):
        return _dot(xn, w_ref[:, idx * width:(idx + 1) * width])

    def rope(t):
        partner = jnp.where(first_half, pltpu.roll(t, width - HEAD // 2, 1),
                            pltpu.roll(t, HEAD // 2, 1))
        return t * cos + partner * sin

    q_ref[...] = (rope(proj(1)) * (HEAD ** -0.5 * LOG2_E)).astype(BF16)
    k_ref[...] = rope(proj(2)).astype(BF16)
    v_ref[...] = proj(3).astype(BF16)

    tile_idx = pl.program_id(0) % tiles_per_seq
    u = proj(0)
    xh = jnp.where(tile_idx == 0, 0.0, _rms(halo_ref[...], g)).astype(BF16)
    ext = jnp.concatenate([_dot(xh, w_ref[:, :width]), u], axis=0)
    pos = tile_idx * tm + lax.broadcasted_iota(jnp.int32, (tm, 1), 0)
    for gi, win in enumerate(POOL_WINDOWS):
        sl = slice(gi * GROUP, (gi + 1) * GROUP)
        acc, span = ext[:, sl], 1
        while span < win:
            acc = acc + pltpu.roll(acc, span, 0)
            span *= 2
        inv_count = 1.0 / jnp.minimum(pos + 1, win).astype(F32)
        delta = acc[POOL_HALO:] * inv_count - u[:, sl]
        y = _dot(delta.astype(BF16), pw_ref[gi]) * ps_ref[:, sl]
        pool_ref[:, sl] = y.astype(BF16)


def _even_proj(x2, g, w, cos, sin, pool_w, pool_scale, seq, tm):
    rows, d = x2.shape
    width = w.shape[1] // 4
    tiles_per_seq = seq // tm
    halo_blocks = tm // POOL_HALO
    row_spec = pl.BlockSpec((tm, d), lambda i: (i, 0))
    halo_spec = pl.BlockSpec((POOL_HALO, d), lambda i: (jnp.maximum(i * halo_blocks - 1, 0), 0))
    out_spec = pl.BlockSpec((tm, width), lambda i: (i, 0))
    tab_spec = pl.BlockSpec((tm, GROUP), lambda i: (i % tiles_per_seq, 0))
    return pl.pallas_call(
        functools.partial(_even_proj_kernel, tiles_per_seq=tiles_per_seq),
        grid=(rows // tm,),
        in_specs=[row_spec, halo_spec, _const_spec((1, d)), _const_spec(w.shape), tab_spec, tab_spec,
                  _const_spec(pool_w.shape), _const_spec((1, width))],
        out_specs=[out_spec] * 4,
        out_shape=[jax.ShapeDtypeStruct((rows, width), BF16)] * 4,
        compiler_params=_params(("parallel",)),
        name="even_proj_pool",
    )(x2, x2, g, w, cos, sin, pool_w, pool_scale)


def _attn_kernel(lam_ref, q_ref, k_ref, v_ref, sg_ref, o_ref, s_buf, *, lambda_init):
    t = q_ref.shape[0]
    i = pl.program_id(2)
    q = q_ref[...]
    lane = lax.broadcasted_iota(jnp.int32, q.shape, 1)
    zero = jnp.zeros_like(q)
    qs = jnp.concatenate([jnp.where(lane < HEAD, q, zero), jnp.where(lane < HEAD, zero, q)], axis=0)

    def lane_fold(op, run, s):
        for c in range(t // GROUP):
            run = op(run, s[:, c * GROUP:(c + 1) * GROUP])
        return run

    def tile(nblk):
        run = jnp.full((2 * t, GROUP), NEG, F32)
        for j in range(nblk):
            kb = k_ref[j * t:(j + 1) * t, :]
            s = jnp.concatenate([_dot_nt(qs[:t], kb), _dot_nt(qs[t:], kb)], axis=0)
            if j == nblk - 1:
                row = lax.broadcasted_iota(jnp.int32, s.shape, 0)
                col = lax.broadcasted_iota(jnp.int32, s.shape, 1)
                s = jnp.where((col // CHUNK) <= ((row % t) // CHUNK), s, NEG)
            s_buf[j] = s
            run = lane_fold(jnp.maximum, run, s)
        m = jnp.max(run, axis=-1, keepdims=True)

        acc = jnp.zeros((2 * t, 2 * GROUP), F32)
        ones = jnp.ones((t, GROUP), BF16)
        for j in range(nblk):
            pb = jnp.exp2((s_buf[j] - m).astype(BF16))
            v1 = jnp.concatenate([v_ref[j * t:(j + 1) * t, :], ones], axis=1)
            acc = acc + jnp.concatenate([_dot(pb[:t], v1), _dot(pb[t:], v1)], axis=0)
        o = acc[:, :GROUP] / acc[:, GROUP:]
        lv = lam_ref[...]
        lam = (jnp.exp(jnp.sum(lv[0:1] * lv[1:2], axis=-1, keepdims=True))
               - jnp.exp(jnp.sum(lv[2:3] * lv[3:4], axis=-1, keepdims=True)) + lambda_init)
        o = o[:t] - lam * o[t:]
        o = _rms(o, sg_ref[...]) * (1.0 - lambda_init)
        o_ref[...] = o.astype(BF16)

    for c in range(s_buf.shape[0]):
        pl.when(i == c)(functools.partial(tile, c + 1))


def _diff_attention(q, k, v, lam_vecs, subln_g, batch, seq, lambda_init, t):
    rows, width = q.shape
    heads = width // GROUP
    nq = seq // t
    q_spec = pl.BlockSpec((t, GROUP), lambda b, h, i: (b * nq + i, h))
    kv_spec = pl.BlockSpec((seq, GROUP), lambda b, h, i: (b, h))
    return pl.pallas_call(
        functools.partial(_attn_kernel, lambda_init=lambda_init),
        grid=(batch, heads, nq),
        in_specs=[_const_spec(lam_vecs.shape), q_spec, kv_spec, kv_spec, _const_spec((1, GROUP))],
        out_specs=q_spec,
        out_shape=jax.ShapeDtypeStruct((rows, width), BF16),
        scratch_shapes=[pltpu.VMEM((nq, 2 * t, t), F32)],
        compiler_params=_params(("parallel", "parallel", "arbitrary")),
        name="diff_attention",
    )(lam_vecs, q, k, v, subln_g)


def _ffn_kernel(*refs, n_acts, tiles_per_seq, tf, final_norm):
    x_ref, halo_ref = refs[0], refs[1]
    act_refs, act_halo_refs = refs[2:2 + n_acts], refs[2 + n_acts:2 + 2 * n_acts]
    wm_ref, g_ref, wi_ref, cw_ref, cb_ref, wo_ref, fg_ref, o_ref, abuf, hbuf = refs[2 + 2 * n_acts:]
    tm = x_ref.shape[0]
    d_ff = hbuf.shape[1]

    def mixed(res, acts):
        row0 = 0
        for a in acts:
            res = res + _dot(a, wm_ref[row0:row0 + a.shape[1], :])[-res.shape[0]:]
            row0 += a.shape[1]
        return res

    x = mixed(x_ref[...], [a[...] for a in act_refs])
    g = g_ref[...]
    xn = _rms(x, g).astype(BF16)
    seq_start = (pl.program_id(0) % tiles_per_seq) == 0
    x_halo = mixed(halo_ref[...], [a[...] for a in act_halo_refs])
    xh = jnp.where(seq_start, 0.0, _rms(x_halo, g)).astype(BF16)

    for c in range(d_ff // tf):
        cols = slice(c * tf, (c + 1) * tf)
        slot = c % 2
        wa = wi_ref[:, cols]
        abuf[slot, 0:HALO, :] = _dot(xh, wa)
        a = _dot(xn, wa)
        abuf[slot, HALO:, :] = a
        gate = _dot(xn, wi_ref[:, d_ff + c * tf:d_ff + (c + 1) * tf])
        cw = cw_ref[:, cols]
        conv = (abuf[slot, HALO - 2:HALO - 2 + tm, :] * cw[0:1]
                + abuf[slot, HALO - 1:HALO - 1 + tm, :] * cw[1:2] + a * cw[2:3] + cb_ref[:, cols])
        act = 0.5 * conv * (1.0 + lax.erf(conv * (2.0 ** -0.5)))
        hbuf[:, cols] = (act * gate).astype(BF16)
    out = x + _dot(hbuf[...], wo_ref[...])
    if final_norm:
        out = _rms(out, fg_ref[...])
    o_ref[...] = out


def _ffn(x2, acts, w_mix, g, w_in, conv_w, conv_b, w_out, final_g, layer, seq, tm, tf, final_norm):
    rows, d = x2.shape
    d_ff = w_out.shape[1]
    tiles_per_seq = seq // tm
    row_spec = pl.BlockSpec((tm, d), lambda i: (i, 0))
    halo_spec = pl.BlockSpec((HALO, d), lambda i: (jnp.maximum(i * (tm // HALO) - 1, 0), 0))
    act_specs = [pl.BlockSpec((tm, a.shape[1]), lambda i: (i, 0)) for a in acts]
    act_halo_specs = [pl.BlockSpec((2 * HALO, a.shape[1]),
                                   lambda i: (jnp.maximum(i * (tm // (2 * HALO)) - 1, 0), 0)) for a in acts]
    return pl.pallas_call(
        functools.partial(_ffn_kernel, n_acts=len(acts), tiles_per_seq=tiles_per_seq, tf=tf,
                          final_norm=final_norm),
        grid=(rows // tm,),
        in_specs=[row_spec, halo_spec] + act_specs + act_halo_specs
        + [_const_spec(w_mix.shape), _const_spec((1, d)), _layer_spec(w_in.shape, layer),
           _const_spec(conv_w.shape), _const_spec((1, d_ff)), _layer_spec(w_out.shape, layer),
           _const_spec((1, d))],
        out_specs=row_spec,
        out_shape=jax.ShapeDtypeStruct((rows, d), F32),
        scratch_shapes=[pltpu.VMEM((2, HALO + tm, tf), F32), pltpu.VMEM((tm, d_ff), BF16)],
        compiler_params=_params(("parallel",)),
        name="mix_proj_conv_ffn",
    )(x2, x2, *acts, *acts, w_mix, g, w_in, conv_w, conv_b, w_out, final_g)


def _rwkv_proj_kernel(x_ref, halo_ref, g_ref, mu_ref, wr_ref, wk_ref, wv_ref, w0_ref, w1_ref,
                      w2_ref, a0_ref, a1_ref, a2_ref, g1_ref, g2_ref, kk_ref, ka_ref,
                      r_out, ld_out, k_out, v_out, kk_out, a_out, g_out, *, tiles_per_seq, tn):
    g = g_ref[...]
    h = _rms(x_ref[...], g)
    seq_start = (pl.program_id(0) % tiles_per_seq) == 0
    h_halo = jnp.where(seq_start, 0.0, _rms(halo_ref[...], g))
    xx = _shift_rows(h, h_halo, 1) - h
    hb, xb, mu = h.astype(BF16), xx.astype(BF16), mu_ref[...].astype(BF16)

    def mix(idx):
        return hb + xb * mu[idx:idx + 1, :]

    xr, xk, xv = mix(0), mix(2), mix(3)
    w_hid = jnp.tanh(_dot(mix(1), w1_ref[...])).astype(BF16)
    a_hid = _dot(mix(4), a1_ref[...]).astype(BF16)
    g_hid = jax.nn.sigmoid(_dot(mix(5), g1_ref[...])).astype(BF16)
    for c in range(h.shape[1] // tn):
        cols = slice(c * tn, (c + 1) * tn)
        r_out[:, cols] = _dot(xr, wr_ref[:, cols]).astype(BF16)
        w_lin = w0_ref[:, cols] + _dot(w_hid, w2_ref[:, cols])
        w_log = -jax.nn.softplus(-w_lin) - 0.5
        ld_out[:, cols] = -jnp.exp(w_log)
        k = _dot(xk, wk_ref[:, cols])
        v_out[:, cols] = _dot(xv, wv_ref[:, cols]).astype(BF16)
        a = jax.nn.sigmoid(a0_ref[:, cols] + _dot(a_hid, a2_ref[:, cols]))
        a_out[:, cols] = a.astype(BF16)
        g_out[:, cols] = _dot(g_hid, g2_ref[:, cols]).astype(BF16)
        kk_out[:, cols] = (k * kk_ref[:, cols]).astype(BF16)
        k_out[:, cols] = (k * (1.0 + (a - 1.0) * ka_ref[:, cols])).astype(BF16)


def _rwkv_proj(x2, g, mu, wr, wk, wv, w0, w1, w2, a0, a1, a2, g1, g2, k_k, k_a, seq, tm):
    rows, d = x2.shape
    tiles_per_seq = seq // tm
    halo_blocks = tm // HALO
    row_spec = pl.BlockSpec((tm, d), lambda i: (i, 0))
    halo_spec = pl.BlockSpec((HALO, d), lambda i: (jnp.maximum(i * halo_blocks - 1, 0), 0))
    vec = _const_spec((1, d))
    consts = [mu, wr, wk, wv, w0, w1, w2, a0, a1, a2, g1, g2, k_k, k_a]
    return pl.pallas_call(
        functools.partial(_rwkv_proj_kernel, tiles_per_seq=tiles_per_seq, tn=256),
        grid=(rows // tm,),
        in_specs=[row_spec, halo_spec, vec] + [_const_spec(c.shape) for c in consts],
        out_specs=[row_spec] * 7,
        out_shape=[jax.ShapeDtypeStruct((rows, d), F32 if i == 1 else BF16) for i in range(7)],
        compiler_params=_params(("parallel",)),
        name="rwkv_proj",
    )(x2, x2, g, *consts)


def _rwkv_rec_kernel(r_ref, ld_ref, k_ref, v_ref, kk_ref, a_ref, g_ref, rk_ref, lw_ref, lb_ref,
                     o_ref, state):
    L = RWKV_L
    seqs = r_ref.shape[0]
    chunks = r_ref.shape[1] // L
    pairs = r_ref.shape[2] // GROUP
    lanes = seqs * pairs

    @pl.when(pl.program_id(2) == 0)
    def _():
        state[...] = jnp.zeros(state.shape, F32)

    def stack(t):
        return jnp.stack([t[q, c * L:(c + 1) * L, p * GROUP:(p + 1) * GROUP]
                          for c in range(chunks) for q in range(seqs) for p in range(pairs)], axis=0)

    def per_lane(ref, reps):
        return jnp.stack([ref[:, p * GROUP:(p + 1) * GROUP] for p in range(pairs)] * (reps * seqs), axis=0)

    head0 = lax.broadcasted_iota(jnp.int32, (1, L, GROUP), 2) < HEAD

    def split(t):
        return jnp.concatenate([jnp.where(head0, t, 0.0), jnp.where(head0, 0.0, t)], axis=1)

    ones_r = lax.broadcasted_iota(jnp.int32, (GROUP, GROUP), 0) // HEAD
    ones_c = lax.broadcasted_iota(jnp.int32, (GROUP, GROUP), 1) // HEAD
    head_ones = (ones_r == ones_c).astype(BF16)

    def head_sum(t):
        n = t.shape[0]
        return _dot(t.reshape(n * L, GROUP).astype(BF16), head_ones).reshape(n, L, GROUP)

    row = lax.broadcasted_iota(jnp.int32, (1, 4 * L, 2 * L), 1)
    col = lax.broadcasted_iota(jnp.int32, (1, 4 * L, 2 * L), 2)
    same_head = ((row // L) % 2) == (col // L)
    strict = (col % L) < (row % L)
    keep = same_head & (strict | ((row >= 2 * L) & ((col % L) == (row % L))))
    r2 = lax.broadcasted_iota(jnp.int32, (1, 2 * L, 2 * L), 1)
    c2 = lax.broadcasted_iota(jnp.int32, (1, 2 * L, 2 * L), 2)
    eye = (r2 == c2).astype(F32)
    block_diag = (r2 // HEAD) == (c2 // HEAD)
    tl = lax.broadcasted_iota(jnp.int32, (chunks * L, chunks * L), 0)
    tc = lax.broadcasted_iota(jnp.int32, (chunks * L, chunks * L), 1)
    tri = ((tc <= tl) & ((tc // L) == (tl // L))).astype(BF16)

    lnx_w, lnx_b, r_k = per_lane(lw_ref, 1), per_lane(lb_ref, 1), per_lane(rk_ref, chunks)

    def load(ref):
        return stack(ref[...]).astype(F32)

    ld3 = ld_ref[...]
    width = ld3.shape[2]
    ld_hi = ld3.astype(BF16)
    ld_lo = (ld3 - ld_hi.astype(F32)).astype(BF16)
    cum = []
    for q in range(seqs):
        c2x = _dot(tri, jnp.concatenate([ld_hi[q], ld_lo[q]], axis=1))
        cum.append(c2x[:, :width] + c2x[:, width:])
    cum = stack(jnp.stack(cum, axis=0))
    ld2 = ld3

    r, ld, kx, v, kk, a = load(r_ref), stack(ld2), load(k_ref), load(v_ref), load(kk_ref), load(a_ref)
    kkn = kk * jnp.minimum(lax.rsqrt(head_sum(kk * kk)), 1e12)
    p_in = jnp.exp(cum)
    inv_p = jnp.exp(-cum)
    rt = r * p_in
    at = -kkn * jnp.exp(cum - ld)
    bt = (kkn * a * inv_p).astype(BF16)
    kt = (kx * inv_p).astype(BF16)

    lhs = jnp.concatenate([at, rt], axis=1).astype(BF16)
    xbk = _bdot_nt(lhs, jnp.concatenate([split(bt), split(kt)], axis=1))
    xa, xr = xbk[:, :L], xbk[:, L:]
    xb = jnp.where(keep, jnp.concatenate([xa[:, :, :2 * L]] * 2 + [xr[:, :, :2 * L]] * 2, axis=1), 0.0)
    xk = jnp.where(keep, jnp.concatenate([xa[:, :, 2 * L:]] * 2 + [xr[:, :, 2 * L:]] * 2, axis=1), 0.0)
    m_ab = xb[:, :2 * L]
    m_r = jnp.concatenate([xb[:, 2 * L:], xk[:, 2 * L:]], axis=2).astype(BF16)
    bk = jnp.concatenate([bt, kt], axis=1)
    v_st = split(v).astype(BF16)
    p_last = p_in[:, L - 1:L, :]
    bonus = head_sum(r * kx * r_k) * v
    gate = load(g_ref)

    n = _bdot(m_ab.astype(BF16), m_ab.astype(BF16))
    t_inv = eye + m_ab
    levels = int(math.log2(L))
    for level in range(1, levels):
        nb = n.astype(BF16)
        if level < levels - 1:
            prod = _bdot(nb, jnp.concatenate([nb, t_inv.astype(BF16)], axis=2))
            n, t_inv = prod[:, :, :2 * L], t_inv + prod[:, :, 2 * L:]
        else:
            t_inv = t_inv + _bdot(nb, t_inv.astype(BF16))
    t_inv = t_inv.astype(BF16)
    akv = _bdot(xk[:, :2 * L].astype(BF16), v_st)

    s_bd = state[...]
    for c in range(chunks):
        u = slice(c * lanes, (c + 1) * lanes)
        xs = _bdot_nt(lhs[u], s_bd.astype(BF16))
        sa_st = _bdot(t_inv[u], (split(xs[:, :L]) + akv[u]).astype(BF16))
        y_st = _bdot(m_r[u], jnp.concatenate([sa_st.astype(BF16), v_st[u]], axis=1))
        y = xs[:, L:] + y_st[:, :L] + y_st[:, L:]
        sa = sa_st[:, :L] + sa_st[:, L:]
        ds = _bdot_tn(jnp.concatenate([sa, v[u]], axis=1).astype(BF16), bk[u])
        s_bd = jnp.where(block_diag, s_bd + ds, 0.0) * p_last[u]

        mean = head_sum(y) * (1.0 / HEAD)
        dev = y - mean
        var = head_sum(dev * dev) * (1.0 / HEAD)
        yn = dev * lax.rsqrt(var + RWKV_GN_EPS) * lnx_w + lnx_b
        out = ((yn + bonus[u]) * gate[u]).astype(BF16)
        for q in range(seqs):
            for p in range(pairs):
                o_ref[q, c * L:(c + 1) * L, p * GROUP:(p + 1) * GROUP] = out[q * pairs + p]
    state[...] = s_bd


def _rwkv_rec(r, ld, k, v, kk, a, g, r_k, lnx_w, lnx_b, batch, seq, pairs, chunks, seqs):
    rows, d = r.shape
    width = pairs * GROUP
    groups = d // width
    nc = seq // (chunks * RWKV_L)
    spec = pl.BlockSpec((seqs, chunks * RWKV_L, width), lambda b, p, c: (b, c, p))
    vec = pl.BlockSpec((1, width), lambda b, p, c: (0, p))
    per_seq = [t.reshape(batch, seq, d) for t in (r, ld, k, v, kk, a, g)]
    out = pl.pallas_call(
        _rwkv_rec_kernel,
        grid=(batch // seqs, groups, nc),
        in_specs=[spec] * 7 + [vec] * 3,
        out_specs=spec,
        out_shape=jax.ShapeDtypeStruct((batch, seq, d), BF16),
        scratch_shapes=[pltpu.VMEM((seqs * pairs, GROUP, GROUP), F32)],
        compiler_params=_params(("parallel", "parallel", "arbitrary")),
        name="rwkv_recurrence",
    )(*per_seq, r_k, lnx_w, lnx_b)
    return out.reshape(rows, d)


def _rotary_tables(seq):
    inv_freq = 1.0 / (ROPE_THETA ** (jnp.arange(0, HEAD, 2, dtype=F32) / HEAD))
    ang = jnp.arange(seq, dtype=F32)[:, None] * inv_freq[None, :]
    cos, sin = jnp.cos(ang), jnp.sin(ang)
    cos = jnp.concatenate([cos, cos, cos, cos], axis=1)
    sin = jnp.concatenate([-sin, sin, -sin, sin], axis=1)
    return cos, sin


def _pad_cols(w):
    return jnp.pad(w, ((0, 0), (0, LORA_PAD - w.shape[1])))


def _pad_rows(w):
    return jnp.pad(w, ((0, LORA_PAD - w.shape[0]), (0, 0)))


def kernel(x, e_norm_g, e_w_in, e_pool_w, e_pool_scale, e_lambda, e_subln_g, e_w_o, o_norm_g, o_mu, o_w_r, o_w_k, o_w_v, o_w_o, o_w0, o_w1, o_w2, o_a0, o_a1, o_a2, o_g1, o_g2, o_k_k, o_k_a, o_r_k, o_lnx_w, o_lnx_b, f_norm_g, f_w_in, f_conv_w, f_conv_b, f_w_out, final_g):
    batch, seq, d = x.shape
    depth = f_norm_g.shape[0]
    tm = min(512, seq)
    attn_t = min(512, seq)
    x2 = x.reshape(batch * seq, d)
    row = lambda vec: vec.reshape(1, -1)
    cos, sin = _rotary_tables(seq)

    f_w_in_b, f_w_out_b = f_w_in.astype(BF16), f_w_out.astype(BF16)

    for layer in range(depth):
        i = layer // 2
        if layer % 2 == 0:
            lambda_init = 0.8 - 0.6 * math.exp(-0.3 * layer)
            pool_out, q, k, v = _even_proj(x2, row(e_norm_g[i]), e_w_in[i].astype(BF16), cos, sin,
                                           e_pool_w[i].astype(BF16), row(e_pool_scale[i]), seq, tm)
            attn_out = _diff_attention(q, k, v, e_lambda[i], row(e_subln_g[i]), batch, seq,
                                       lambda_init, attn_t)
            acts, w_mix = [pool_out, attn_out], e_w_o[i].astype(BF16)
        else:
            r, ld, k, v, kk, a, g = _rwkv_proj(
                x2, row(o_norm_g[i]), o_mu[i], o_w_r[i].astype(BF16), o_w_k[i].astype(BF16),
                o_w_v[i].astype(BF16), row(o_w0[i]), _pad_cols(o_w1[i]).astype(BF16),
                _pad_rows(o_w2[i]).astype(BF16), row(o_a0[i]), _pad_cols(o_a1[i]).astype(BF16),
                _pad_rows(o_a2[i]).astype(BF16), o_g1[i].astype(BF16), o_g2[i].astype(BF16),
                row(o_k_k[i]), row(o_k_a[i]), seq, tm)
            z = _rwkv_rec(r, ld, k, v, kk, a, g, row(o_r_k[i]), row(o_lnx_w[i]), row(o_lnx_b[i]),
                          batch, seq, pairs=8, chunks=4, seqs=2 if batch % 2 == 0 else 1)
            acts, w_mix = [z], o_w_o[i].astype(BF16)
        x2 = _ffn(x2, acts, w_mix, row(f_norm_g[layer]), f_w_in_b, f_conv_w[layer],
                  row(f_conv_b[layer]), f_w_out_b, row(final_g), layer, seq,
                  min(1024, seq), tf=256, final_norm=(layer == depth - 1))
    return x2.reshape(batch, seq, d)
```

```python
import functools
import math

import jax
import jax.numpy as jnp
from jax import lax
from jax.experimental import pallas as pl
from jax.experimental.pallas import tpu as pltpu

F32 = jnp.float32
BF16 = jnp.bfloat16

RMS_EPS = 1e-5
ROPE_THETA = 10000.0
CHUNK = 64
POOL_WINDOWS = (2, 4, 8, 16)
POOL_HALO = 16
GROUP = 128
HEAD = 64
RWKV_GN_EPS = 64e-5
RWKV_L = 64
LORA_PAD = 128
HALO = 8
NEG = -1e30
LOG2_E = 1.4426950408889634
VMEM_LIMIT = 56 * 1024 * 1024


def _rms(x, g):
    return x * lax.rsqrt(jnp.mean(x * x, axis=-1, keepdims=True) + RMS_EPS) * g


def _dot(a, b):
    return jnp.dot(a, b, preferred_element_type=F32)


def _dot_nt(a, b):
    return lax.dot_general(a, b, (((1,), (1,)), ((), ())), preferred_element_type=F32)


def _dot_tn(a, b):
    return lax.dot_general(a, b, (((0,), (0,)), ((), ())), preferred_element_type=F32)


def _bdot(a, b):
    return lax.dot_general(a, b, (((2,), (1,)), ((0,), (0,))), preferred_element_type=F32)


def _bdot_nt(a, b):
    return lax.dot_general(a, b, (((2,), (2,)), ((0,), (0,))), preferred_element_type=F32)


def _bdot_tn(a, b):
    return lax.dot_general(a, b, (((1,), (1,)), ((0,), (0,))), preferred_element_type=F32)


def _shift_rows(t, halo, n):
    rolled = pltpu.roll(t, n, 0)
    row = lax.broadcasted_iota(jnp.int32, halo.shape, 0)
    head = jnp.where(row < n, pltpu.roll(halo, n, 0), rolled[:HALO])
    return jnp.concatenate([head, rolled[HALO:]], axis=0)


def _const_spec(shape):
    return pl.BlockSpec(shape, lambda *_: (0,) * len(shape), pipeline_mode=pl.Buffered(1))


def _layer_spec(shape, layer):
    return pl.BlockSpec((None,) + tuple(shape[1:]), lambda *_: (layer,) + (0,) * (len(shape) - 1),
                        pipeline_mode=pl.Buffered(1))


def _params(sem):
    return pltpu.CompilerParams(dimension_semantics=sem, vmem_limit_bytes=VMEM_LIMIT)


def _even_proj_kernel(x_ref, halo_ref, g_ref, w_ref, cos_ref, sin_ref, pw_ref, ps_ref,
                      pool_ref, q_ref, k_ref, v_ref, *, tiles_per_seq):
    tm, width = q_ref.shape
    g = g_ref[...]
    xn = _rms(x_ref[...], g).astype(BF16)
    cos = jnp.concatenate([cos_ref[...]] * (width // GROUP), axis=1)
    sin = jnp.concatenate([sin_ref[...]] * (width // GROUP), axis=1)
    lane = lax.broadcasted_iota(jnp.int32, cos.shape, 1)
    first_half = (lane % HEAD) < HEAD // 2

    def proj(idx):
        return _dot(xn, w_ref[:, idx * width:(idx + 1) * width])

    def rope(t):
        partner = jnp.where(first_half, pltpu.roll(t, width - HEAD // 2, 1),
                            pltpu.roll(t, HEAD // 2, 1))
        return t * cos + partner * sin

    q_ref[...] = (rope(proj(1)) * (HEAD ** -0.5 * LOG2_E)).astype(BF16)
    k_ref[...] = rope(proj(2)).astype(BF16)
    v_ref[...] = proj(3).astype(BF16)

    tile_idx = pl.program_id(0) % tiles_per_seq
    u = proj(0)
    xh = jnp.where(tile_idx == 0, 0.0, _rms(halo_ref[...], g)).astype(BF16)
    ext = jnp.concatenate([_dot(xh, w_ref[:, :width]), u], axis=0)
    pos = tile_idx * tm + lax.broadcasted_iota(jnp.int32, (tm, 1), 0)
    for gi, win in enumerate(POOL_WINDOWS):
        sl = slice(gi * GROUP, (gi + 1) * GROUP)
        acc, span = ext[:, sl], 1
        while span < win:
            acc = acc + pltpu.roll(acc, span, 0)
            span *= 2
        inv_count = 1.0 / jnp.minimum(pos + 1, win).astype(F32)
        delta = acc[POOL_HALO:] * inv_count - u[:, sl]
        y = _dot(delta.astype(BF16), pw_ref[gi]) * ps_ref[:, sl]
        pool_ref[:, sl] = y.astype(BF16)


def _even_proj(x2, g, w, cos, sin, pool_w, pool_scale, seq, tm):
    rows, d = x2.shape
    width = w.shape[1] // 4
    tiles_per_seq = seq // tm
    halo_blocks = tm // POOL_HALO
    row_spec = pl.BlockSpec((tm, d), lambda i: (i, 0))
    halo_spec = pl.BlockSpec((POOL_HALO, d), lambda i: (jnp.maximum(i * halo_blocks - 1, 0), 0))
    out_spec = pl.BlockSpec((tm, width), lambda i: (i, 0))
    tab_spec = pl.BlockSpec((tm, GROUP), lambda i: (i % tiles_per_seq, 0))
    return pl.pallas_call(
        functools.partial(_even_proj_kernel, tiles_per_seq=tiles_per_seq),
        grid=(rows // tm,),
        in_specs=[row_spec, halo_spec, _const_spec((1, d)), _const_spec(w.shape), tab_spec, tab_spec,
                  _const_spec(pool_w.shape), _const_spec((1, width))],
        out_specs=[out_spec] * 4,
        out_shape=[jax.ShapeDtypeStruct((rows, width), BF16)] * 4,
        compiler_params=_params(("parallel",)),
        name="even_proj_pool",
    )(x2, x2, g, w, cos, sin, pool_w, pool_scale)


def _attn_kernel(lam_ref, q_ref, k_ref, v_ref, sg_ref, o_ref, s_buf, *, lambda_init):
    t = q_ref.shape[0]
    i = pl.program_id(2)
    q = q_ref[...]
    lane = lax.broadcasted_iota(jnp.int32, q.shape, 1)
    zero = jnp.zeros_like(q)
    qs = jnp.concatenate([jnp.where(lane < HEAD, q, zero), jnp.where(lane < HEAD, zero, q)], axis=0)

    def lane_fold(op, run, s):
        for c in range(t // GROUP):
            run = op(run, s[:, c * GROUP:(c + 1) * GROUP])
        return run

    def tile(nblk):
        half = t // 2
        run = jnp.full((2 * t, GROUP), NEG, F32)
        for j in range(nblk):
            kb = k_ref[j * t:(j + 1) * t, :]
            if j < nblk - 1:
                s = jnp.concatenate([_dot_nt(qs[:t], kb), _dot_nt(qs[t:], kb)], axis=0)
            else:
                hidden = jnp.full((half, half), NEG, F32)
                parts = []
                for c in range(2):
                    qc = qs[c * t:(c + 1) * t]
                    parts += [jnp.concatenate([_dot_nt(qc[:half], kb[:half]), hidden], axis=1),
                              _dot_nt(qc[half:], kb)]
                s = jnp.concatenate(parts, axis=0)
                row = lax.broadcasted_iota(jnp.int32, s.shape, 0)
                col = lax.broadcasted_iota(jnp.int32, s.shape, 1)
                s = jnp.where((col // CHUNK) <= ((row % t) // CHUNK), s, NEG)
            s_buf[j] = s
            run = lane_fold(jnp.maximum, run, s)
        m = jnp.max(run, axis=-1, keepdims=True)

        def probs(j, rows, cols):
            return jnp.exp2((s_buf[j, rows, cols] - m[rows]).astype(BF16))

        acc = jnp.zeros((2 * t, 2 * GROUP), F32)
        ones = jnp.ones((t, GROUP), BF16)
        for j in range(nblk):
            v1 = jnp.concatenate([v_ref[j * t:(j + 1) * t, :], ones], axis=1)
            if j < nblk - 1:
                parts = [_dot(probs(j, slice(c * t, (c + 1) * t), slice(0, t)), v1) for c in range(2)]
            else:
                parts = []
                for c in range(2):
                    parts += [_dot(probs(j, slice(c * t, c * t + half), slice(0, half)), v1[:half]),
                              _dot(probs(j, slice(c * t + half, (c + 1) * t), slice(0, t)), v1)]
            acc = acc + jnp.concatenate(parts, axis=0)
        o = acc[:, :GROUP] / acc[:, GROUP:]
        lv = lam_ref[...]
        lam = (jnp.exp(jnp.sum(lv[0:1] * lv[1:2], axis=-1, keepdims=True))
               - jnp.exp(jnp.sum(lv[2:3] * lv[3:4], axis=-1, keepdims=True)) + lambda_init)
        o = o[:t] - lam * o[t:]
        o = _rms(o, sg_ref[...]) * (1.0 - lambda_init)
        o_ref[...] = o.astype(BF16)

    for c in range(s_buf.shape[0]):
        pl.when(i == c)(functools.partial(tile, c + 1))


def _diff_attention(q, k, v, lam_vecs, subln_g, batch, seq, lambda_init, t):
    rows, width = q.shape
    heads = width // GROUP
    nq = seq // t
    q_spec = pl.BlockSpec((t, GROUP), lambda b, h, i: (b * nq + i, h))
    kv_spec = pl.BlockSpec((seq, GROUP), lambda b, h, i: (b, h))
    return pl.pallas_call(
        functools.partial(_attn_kernel, lambda_init=lambda_init),
        grid=(batch, heads, nq),
        in_specs=[_const_spec(lam_vecs.shape), q_spec, kv_spec, kv_spec, _const_spec((1, GROUP))],
        out_specs=q_spec,
        out_shape=jax.ShapeDtypeStruct((rows, width), BF16),
        scratch_shapes=[pltpu.VMEM((nq, 2 * t, t), F32)],
        compiler_params=_params(("parallel", "parallel", "arbitrary")),
        name="diff_attention",
    )(lam_vecs, q, k, v, subln_g)


def _ffn_kernel(*refs, n_acts, tiles_per_seq, tf, final_norm):
    x_ref, halo_ref = refs[0], refs[1]
    act_refs, act_halo_refs = refs[2:2 + n_acts], refs[2 + n_acts:2 + 2 * n_acts]
    wm_ref, g_ref, wi_ref, cw_ref, cb_ref, wo_ref, fg_ref, o_ref, abuf, hbuf = refs[2 + 2 * n_acts:]
    tm = x_ref.shape[0]
    d_ff = hbuf.shape[1]

    def mixed(res, acts):
        row0 = 0
        for a in acts:
            res = res + _dot(a, wm_ref[row0:row0 + a.shape[1], :])[-res.shape[0]:]
            row0 += a.shape[1]
        return res

    x = mixed(x_ref[...], [a[...] for a in act_refs])
    g = g_ref[...]
    xn = _rms(x, g).astype(BF16)
    seq_start = (pl.program_id(0) % tiles_per_seq) == 0
    x_halo = mixed(halo_ref[...], [a[...] for a in act_halo_refs])
    xh = jnp.where(seq_start, 0.0, _rms(x_halo, g)).astype(BF16)

    for c in range(d_ff // tf):
        cols = slice(c * tf, (c + 1) * tf)
        slot = c % 2
        wa = wi_ref[:, cols]
        abuf[slot, 0:HALO, :] = _dot(xh, wa)
        a = _dot(xn, wa)
        abuf[slot, HALO:, :] = a
        gate = _dot(xn, wi_ref[:, d_ff + c * tf:d_ff + (c + 1) * tf])
        cw = cw_ref[:, cols]
        conv = (abuf[slot, HALO - 2:HALO - 2 + tm, :] * cw[0:1]
                + abuf[slot, HALO - 1:HALO - 1 + tm, :] * cw[1:2] + a * cw[2:3] + cb_ref[:, cols])
        act = 0.5 * conv * (1.0 + lax.erf(conv * (2.0 ** -0.5)))
        hbuf[:, cols] = (act * gate).astype(BF16)
    out = x + _dot(hbuf[...], wo_ref[...])
    if final_norm:
        out = _rms(out, fg_ref[...])
    o_ref[...] = out


def _ffn(x2, acts, w_mix, g, w_in, conv_w, conv_b, w_out, final_g, layer, seq, tm, tf, final_norm):
    rows, d = x2.shape
    d_ff = w_out.shape[1]
    tiles_per_seq = seq // tm
    row_spec = pl.BlockSpec((tm, d), lambda i: (i, 0))
    halo_spec = pl.BlockSpec((HALO, d), lambda i: (jnp.maximum(i * (tm // HALO) - 1, 0), 0))
    act_specs = [pl.BlockSpec((tm, a.shape[1]), lambda i: (i, 0)) for a in acts]
    act_halo_specs = [pl.BlockSpec((2 * HALO, a.shape[1]),
                                   lambda i: (jnp.maximum(i * (tm // (2 * HALO)) - 1, 0), 0)) for a in acts]
    return pl.pallas_call(
        functools.partial(_ffn_kernel, n_acts=len(acts), tiles_per_seq=tiles_per_seq, tf=tf,
                          final_norm=final_norm),
        grid=(rows // tm,),
        in_specs=[row_spec, halo_spec] + act_specs + act_halo_specs
        + [_const_spec(w_mix.shape), _const_spec((1, d)), _layer_spec(w_in.shape, layer),
           _const_spec(conv_w.shape), _const_spec((1, d_ff)), _layer_spec(w_out.shape, layer),
           _const_spec((1, d))],
        out_specs=row_spec,
        out_shape=jax.ShapeDtypeStruct((rows, d), F32),
        scratch_shapes=[pltpu.VMEM((2, HALO + tm, tf), F32), pltpu.VMEM((tm, d_ff), BF16)],
        compiler_params=_params(("parallel",)),
        name="mix_proj_conv_ffn",
    )(x2, x2, *acts, *acts, w_mix, g, w_in, conv_w, conv_b, w_out, final_g)


def _rwkv_proj_kernel(x_ref, halo_ref, g_ref, mu_ref, wr_ref, wk_ref, wv_ref, w0_ref, w1_ref,
                      w2_ref, a0_ref, a1_ref, a2_ref, g1_ref, g2_ref, kk_ref, ka_ref,
                      r_out, ld_out, k_out, v_out, kk_out, a_out, g_out, *, tiles_per_seq, tn):
    g = g_ref[...]
    h = _rms(x_ref[...], g)
    seq_start = (pl.program_id(0) % tiles_per_seq) == 0
    h_halo = jnp.where(seq_start, 0.0, _rms(halo_ref[...], g))
    xx = _shift_rows(h, h_halo, 1) - h
    hb, xb, mu = h.astype(BF16), xx.astype(BF16), mu_ref[...].astype(BF16)

    def mix(idx):
        return hb + xb * mu[idx:idx + 1, :]

    xr, xk, xv = mix(0), mix(2), mix(3)
    w_hid = jnp.tanh(_dot(mix(1), w1_ref[...])).astype(BF16)
    a_hid = _dot(mix(4), a1_ref[...]).astype(BF16)
    g_hid = jax.nn.sigmoid(_dot(mix(5), g1_ref[...])).astype(BF16)
    for c in range(h.shape[1] // tn):
        cols = slice(c * tn, (c + 1) * tn)
        r_out[:, cols] = _dot(xr, wr_ref[:, cols]).astype(BF16)
        w_lin = w0_ref[:, cols] + _dot(w_hid, w2_ref[:, cols])
        w_log = -jax.nn.softplus(-w_lin) - 0.5
        ld_out[:, cols] = -jnp.exp(w_log)
        k = _dot(xk, wk_ref[:, cols])
        v_out[:, cols] = _dot(xv, wv_ref[:, cols]).astype(BF16)
        a = jax.nn.sigmoid(a0_ref[:, cols] + _dot(a_hid, a2_ref[:, cols]))
        a_out[:, cols] = a.astype(BF16)
        g_out[:, cols] = _dot(g_hid, g2_ref[:, cols]).astype(BF16)
        kk_out[:, cols] = (k * kk_ref[:, cols]).astype(BF16)
        k_out[:, cols] = (k * (1.0 + (a - 1.0) * ka_ref[:, cols])).astype(BF16)


def _rwkv_proj(x2, g, mu, wr, wk, wv, w0, w1, w2, a0, a1, a2, g1, g2, k_k, k_a, seq, tm):
    rows, d = x2.shape
    tiles_per_seq = seq // tm
    halo_blocks = tm // HALO
    row_spec = pl.BlockSpec((tm, d), lambda i: (i, 0))
    halo_spec = pl.BlockSpec((HALO, d), lambda i: (jnp.maximum(i * halo_blocks - 1, 0), 0))
    vec = _const_spec((1, d))
    consts = [mu, wr, wk, wv, w0, w1, w2, a0, a1, a2, g1, g2, k_k, k_a]
    return pl.pallas_call(
        functools.partial(_rwkv_proj_kernel, tiles_per_seq=tiles_per_seq, tn=256),
        grid=(rows // tm,),
        in_specs=[row_spec, halo_spec, vec] + [_const_spec(c.shape) for c in consts],
        out_specs=[row_spec] * 7,
        out_shape=[jax.ShapeDtypeStruct((rows, d), F32 if i == 1 else BF16) for i in range(7)],
        compiler_params=_params(("parallel",)),
        name="rwkv_proj",
    )(x2, x2, g, *consts)


def _rwkv_rec_kernel(r_ref, ld_ref, k_ref, v_ref, kk_ref, a_ref, g_ref, rk_ref, lw_ref, lb_ref,
                     o_ref, state):
    L = RWKV_L
    seqs = r_ref.shape[0]
    chunks = r_ref.shape[1] // L
    pairs = r_ref.shape[2] // GROUP
    lanes = seqs * pairs

    @pl.when(pl.program_id(2) == 0)
    def _():
        state[...] = jnp.zeros(state.shape, F32)

    def stack(t):
        return jnp.stack([t[q, c * L:(c + 1) * L, p * GROUP:(p + 1) * GROUP]
                          for c in range(chunks) for q in range(seqs) for p in range(pairs)], axis=0)

    def per_lane(ref, reps):
        return jnp.stack([ref[:, p * GROUP:(p + 1) * GROUP] for p in range(pairs)] * (reps * seqs), axis=0)

    head0 = lax.broadcasted_iota(jnp.int32, (1, L, GROUP), 2) < HEAD

    def split(t):
        return jnp.concatenate([jnp.where(head0, t, 0.0), jnp.where(head0, 0.0, t)], axis=1)

    ones_r = lax.broadcasted_iota(jnp.int32, (GROUP, GROUP), 0) // HEAD
    ones_c = lax.broadcasted_iota(jnp.int32, (GROUP, GROUP), 1) // HEAD
    head_ones = (ones_r == ones_c).astype(BF16)

    def head_sum(t):
        n = t.shape[0]
        return _dot(t.reshape(n * L, GROUP).astype(BF16), head_ones).reshape(n, L, GROUP)

    row = lax.broadcasted_iota(jnp.int32, (1, 4 * L, 2 * L), 1)
    col = lax.broadcasted_iota(jnp.int32, (1, 4 * L, 2 * L), 2)
    same_head = ((row // L) % 2) == (col // L)
    strict = (col % L) < (row % L)
    keep = same_head & (strict | ((row >= 2 * L) & ((col % L) == (row % L))))
    r2 = lax.broadcasted_iota(jnp.int32, (1, 2 * L, 2 * L), 1)
    c2 = lax.broadcasted_iota(jnp.int32, (1, 2 * L, 2 * L), 2)
    eye = (r2 == c2).astype(F32)
    block_diag = (r2 // HEAD) == (c2 // HEAD)
    tl = lax.broadcasted_iota(jnp.int32, (chunks * L, chunks * L), 0)
    tc = lax.broadcasted_iota(jnp.int32, (chunks * L, chunks * L), 1)
    tri = ((tc <= tl) & ((tc // L) == (tl // L))).astype(BF16)

    lnx_w, lnx_b, r_k = per_lane(lw_ref, 1), per_lane(lb_ref, 1), per_lane(rk_ref, chunks)

    def load(ref):
        return stack(ref[...]).astype(F32)

    ld3 = ld_ref[...]
    width = ld3.shape[2]
    ld_hi = ld3.astype(BF16)
    ld_lo = (ld3 - ld_hi.astype(F32)).astype(BF16)
    cum = []
    for q in range(seqs):
        c2x = _dot(tri, jnp.concatenate([ld_hi[q], ld_lo[q]], axis=1))
        cum.append(c2x[:, :width] + c2x[:, width:])
    cum = stack(jnp.stack(cum, axis=0))
    ld2 = ld3

    r, ld, kx, v, kk, a = load(r_ref), stack(ld2), load(k_ref), load(v_ref), load(kk_ref), load(a_ref)
    kkn = kk * jnp.minimum(lax.rsqrt(head_sum(kk * kk)), 1e12)
    p_in = jnp.exp(cum)
    inv_p = jnp.exp(-cum)
    rt = r * p_in
    at = -kkn * jnp.exp(cum - ld)
    bt = (kkn * a * inv_p).astype(BF16)
    kt = (kx * inv_p).astype(BF16)

    lhs = jnp.concatenate([at, rt], axis=1).astype(BF16)
    xbk = _bdot_nt(lhs, jnp.concatenate([split(bt), split(kt)], axis=1))
    xa, xr = xbk[:, :L], xbk[:, L:]
    xb = jnp.where(keep, jnp.concatenate([xa[:, :, :2 * L]] * 2 + [xr[:, :, :2 * L]] * 2, axis=1), 0.0)
    xk = jnp.where(keep, jnp.concatenate([xa[:, :, 2 * L:]] * 2 + [xr[:, :, 2 * L:]] * 2, axis=1), 0.0)
    m_ab = xb[:, :2 * L]
    m_r = jnp.concatenate([xb[:, 2 * L:], xk[:, 2 * L:]], axis=2).astype(BF16)
    bk = jnp.concatenate([bt, kt], axis=1)
    v_st = split(v).astype(BF16)
    p_last = p_in[:, L - 1:L, :]
    bonus = head_sum(r * kx * r_k) * v
    gate = load(g_ref)

    n = _bdot(m_ab.astype(BF16), m_ab.astype(BF16))
    t_inv = eye + m_ab
    levels = int(math.log2(L))
    for level in range(1, levels):
        nb = n.astype(BF16)
        if level < levels - 1:
            prod = _bdot(nb, jnp.concatenate([nb, t_inv.astype(BF16)], axis=2))
            n, t_inv = prod[:, :, :2 * L], t_inv + prod[:, :, 2 * L:]
        else:
            t_inv = t_inv + _bdot(nb, t_inv.astype(BF16))
    t_inv = t_inv.astype(BF16)
    akv = _bdot(xk[:, :2 * L].astype(BF16), v_st)

    s_bd = state[...]
    for c in range(chunks):
        u = slice(c * lanes, (c + 1) * lanes)
        xs = _bdot_nt(lhs[u], s_bd.astype(BF16))
        sa_st = _bdot(t_inv[u], (split(xs[:, :L]) + akv[u]).astype(BF16))
        y_st = _bdot(m_r[u], jnp.concatenate([sa_st.astype(BF16), v_st[u]], axis=1))
        y = xs[:, L:] + y_st[:, :L] + y_st[:, L:]
        sa = sa_st[:, :L] + sa_st[:, L:]
        ds = _bdot_tn(jnp.concatenate([sa, v[u]], axis=1).astype(BF16), bk[u])
        s_bd = jnp.where(block_diag, s_bd + ds, 0.0) * p_last[u]

        mean = head_sum(y) * (1.0 / HEAD)
        dev = y - mean
        var = head_sum(dev * dev) * (1.0 / HEAD)
        yn = dev * lax.rsqrt(var + RWKV_GN_EPS) * lnx_w + lnx_b
        out = ((yn + bonus[u]) * gate[u]).astype(BF16)
        for q in range(seqs):
            for p in range(pairs):
                o_ref[q, c * L:(c + 1) * L, p * GROUP:(p + 1) * GROUP] = out[q * pairs + p]
    state[...] = s_bd


def _rwkv_rec(r, ld, k, v, kk, a, g, r_k, lnx_w, lnx_b, batch, seq, pairs, chunks, seqs):
    rows, d = r.shape
    width = pairs * GROUP
    groups = d // width
    nc = seq // (chunks * RWKV_L)
    spec = pl.BlockSpec((seqs, chunks * RWKV_L, width), lambda b, p, c: (b, c, p))
    vec = pl.BlockSpec((1, width), lambda b, p, c: (0, p))
    per_seq = [t.reshape(batch, seq, d) for t in (r, ld, k, v, kk, a, g)]
    out = pl.pallas_call(
        _rwkv_rec_kernel,
        grid=(batch // seqs, groups, nc),
        in_specs=[spec] * 7 + [vec] * 3,
        out_specs=spec,
        out_shape=jax.ShapeDtypeStruct((batch, seq, d), BF16),
        scratch_shapes=[pltpu.VMEM((seqs * pairs, GROUP, GROUP), F32)],
        compiler_params=_params(("parallel", "parallel", "arbitrary")),
        name="rwkv_recurrence",
    )(*per_seq, r_k, lnx_w, lnx_b)
    return out.reshape(rows, d)


def _rotary_tables(seq):
    inv_freq = 1.0 / (ROPE_THETA ** (jnp.arange(0, HEAD, 2, dtype=F32) / HEAD))
    ang = jnp.arange(seq, dtype=F32)[:, None] * inv_freq[None, :]
    cos, sin = jnp.cos(ang), jnp.sin(ang)
    cos = jnp.concatenate([cos, cos, cos, cos], axis=1)
    sin = jnp.concatenate([-sin, sin, -sin, sin], axis=1)
    return cos, sin


def _pad_cols(w):
    return jnp.pad(w, ((0, 0), (0, LORA_PAD - w.shape[1])))


def _pad_rows(w):
    return jnp.pad(w, ((0, LORA_PAD - w.shape[0]), (0, 0)))


def kernel(x, e_norm_g, e_w_in, e_pool_w, e_pool_scale, e_lambda, e_subln_g, e_w_o, o_norm_g, o_mu, o_w_r, o_w_k, o_w_v, o_w_o, o_w0, o_w1, o_w2, o_a0, o_a1, o_a2, o_g1, o_g2, o_k_k, o_k_a, o_r_k, o_lnx_w, o_lnx_b, f_norm_g, f_w_in, f_conv_w, f_conv_b, f_w_out, final_g):
    batch, seq, d = x.shape
    depth = f_norm_g.shape[0]
    tm = min(512, seq)
    attn_t = min(512, seq)
    x2 = x.reshape(batch * seq, d)
    row = lambda vec: vec.reshape(1, -1)
    cos, sin = _rotary_tables(seq)

    f_w_in_b, f_w_out_b = f_w_in.astype(BF16), f_w_out.astype(BF16)

    for layer in range(depth):
        i = layer // 2
        if layer % 2 == 0:
            lambda_init = 0.8 - 0.6 * math.exp(-0.3 * layer)
            pool_out, q, k, v = _even_proj(x2, row(e_norm_g[i]), e_w_in[i].astype(BF16), cos, sin,
                                           e_pool_w[i].astype(BF16), row(e_pool_scale[i]), seq, tm)
            attn_out = _diff_attention(q, k, v, e_lambda[i], row(e_subln_g[i]), batch, seq,
                                       lambda_init, attn_t)
            acts, w_mix = [pool_out, attn_out], e_w_o[i].astype(BF16)
        else:
            r, ld, k, v, kk, a, g = _rwkv_proj(
                x2, row(o_norm_g[i]), o_mu[i], o_w_r[i].astype(BF16), o_w_k[i].astype(BF16),
                o_w_v[i].astype(BF16), row(o_w0[i]), _pad_cols(o_w1[i]).astype(BF16),
                _pad_rows(o_w2[i]).astype(BF16), row(o_a0[i]), _pad_cols(o_a1[i]).astype(BF16),
                _pad_rows(o_a2[i]).astype(BF16), o_g1[i].astype(BF16), o_g2[i].astype(BF16),
                row(o_k_k[i]), row(o_k_a[i]), seq, tm)
            z = _rwkv_rec(r, ld, k, v, kk, a, g, row(o_r_k[i]), row(o_lnx_w[i]), row(o_lnx_b[i]),
                          batch, seq, pairs=8, chunks=4, seqs=2 if batch % 2 == 0 else 1)
            acts, w_mix = [z], o_w_o[i].astype(BF16)
        x2 = _ffn(x2, acts, w_mix, row(f_norm_g[layer]), f_w_in_b, f_conv_w[layer],
                  row(f_conv_b[layer]), f_w_out_b, row(final_g), layer, seq,
                  min(1024, seq), tf=256, final_norm=(layer == depth - 1))
    return x2.reshape(batch, seq, d)
```

```python
import functools
import math

import jax
import jax.numpy as jnp
from jax import lax
from jax.experimental import pallas as pl
from jax.experimental.pallas import tpu as pltpu

F32 = jnp.float32
BF16 = jnp.bfloat16

RMS_EPS = 1e-5
ROPE_THETA = 10000.0
CHUNK = 64
POOL_WINDOWS = (2, 4, 8, 16)
POOL_HALO = 16
GROUP = 128
HEAD = 64
RWKV_GN_EPS = 64e-5
RWKV_L = 64
LORA_PAD = 128
HALO = 8
NEG = -1e30
LOG2_E = 1.4426950408889634
VMEM_LIMIT = 56 * 1024 * 1024


def _rms(x, g):
    return x * lax.rsqrt(jnp.mean(x * x, axis=-1, keepdims=True) + RMS_EPS) * g


def _dot(a, b):
    return jnp.dot(a, b, preferred_element_type=F32)


def _dot_nt(a, b):
    return lax.dot_general(a, b, (((1,), (1,)), ((), ())), preferred_element_type=F32)


def _dot_tn(a, b):
    return lax.dot_general(a, b, (((0,), (0,)), ((), ())), preferred_element_type=F32)


def _bdot(a, b):
    return lax.dot_general(a, b, (((2,), (1,)), ((0,), (0,))), preferred_element_type=F32)


def _bdot_nt(a, b):
    return lax.dot_general(a, b, (((2,), (2,)), ((0,), (0,))), preferred_element_type=F32)


def _bdot_tn(a, b):
    return lax.dot_general(a, b, (((1,), (1,)), ((0,), (0,))), preferred_element_type=F32)


def _shift_rows(t, halo, n):
    rolled = pltpu.roll(t, n, 0)
    row = lax.broadcasted_iota(jnp.int32, halo.shape, 0)
    head = jnp.where(row < n, pltpu.roll(halo, n, 0), rolled[:HALO])
    return jnp.concatenate([head, rolled[HALO:]], axis=0)


def _const_spec(shape):
    return pl.BlockSpec(shape, lambda *_: (0,) * len(shape), pipeline_mode=pl.Buffered(1))


def _layer_spec(shape, layer):
    return pl.BlockSpec((None,) + tuple(shape[1:]), lambda *_: (layer,) + (0,) * (len(shape) - 1),
                        pipeline_mode=pl.Buffered(1))


def _params(sem):
    return pltpu.CompilerParams(dimension_semantics=sem, vmem_limit_bytes=VMEM_LIMIT)


def _even_proj_kernel(x_ref, halo_ref, g_ref, w_ref, cos_ref, sin_ref, pw_ref, ps_ref,
                      pool_ref, q_ref, k_ref, v_ref, *, tiles_per_seq):
    tm, width = q_ref.shape
    g = g_ref[...]
    xn = _rms(x_ref[...], g).astype(BF16)
    cos = jnp.concatenate([cos_ref[...]] * (width // GROUP), axis=1)
    sin = jnp.concatenate([sin_ref[...]] * (width // GROUP), axis=1)
    lane = lax.broadcasted_iota(jnp.int32, cos.shape, 1)
    first_half = (lane % HEAD) < HEAD // 2

    def proj(idx):
        return _dot(xn, w_ref[:, idx * width:(idx + 1) * width])

    def rope(t):
        partner = jnp.where(first_half, pltpu.roll(t, width - HEAD // 2, 1),
                            pltpu.roll(t, HEAD // 2, 1))
        return t * cos + partner * sin

    q_ref[...] = (rope(proj(1)) * (HEAD ** -0.5 * LOG2_E)).astype(BF16)
    k_ref[...] = rope(proj(2)).astype(BF16)
    v_ref[...] = proj(3).astype(BF16)

    tile_idx = pl.program_id(0) % tiles_per_seq
    u = proj(0)
    xh = jnp.where(tile_idx == 0, 0.0, _rms(halo_ref[...], g)).astype(BF16)
    ext = jnp.concatenate([_dot(xh, w_ref[:, :width]), u], axis=0)
    pos = tile_idx * tm + lax.broadcasted_iota(jnp.int32, (tm, 1), 0)
    for gi, win in enumerate(POOL_WINDOWS):
        sl = slice(gi * GROUP, (gi + 1) * GROUP)
        acc, span = ext[:, sl], 1
        while span < win:
            acc = acc + pltpu.roll(acc, span, 0)
            span *= 2
        inv_count = 1.0 / jnp.minimum(pos + 1, win).astype(F32)
        delta = acc[POOL_HALO:] * inv_count - u[:, sl]
        y = _dot(delta.astype(BF16), pw_ref[gi]) * ps_ref[:, sl]
        pool_ref[:, sl] = y.astype(BF16)


def _even_proj(x2, g, w, cos, sin, pool_w, pool_scale, seq, tm):
    rows, d = x2.shape
    width = w.shape[1] // 4
    tiles_per_seq = seq // tm
    halo_blocks = tm // POOL_HALO
    row_spec = pl.BlockSpec((tm, d), lambda i: (i, 0))
    halo_spec = pl.BlockSpec((POOL_HALO, d), lambda i: (jnp.maximum(i * halo_blocks - 1, 0), 0))
    out_spec = pl.BlockSpec((tm, width), lambda i: (i, 0))
    tab_spec = pl.BlockSpec((tm, GROUP), lambda i: (i % tiles_per_seq, 0))
    return pl.pallas_call(
        functools.partial(_even_proj_kernel, tiles_per_seq=tiles_per_seq),
        grid=(rows // tm,),
        in_specs=[row_spec, halo_spec, _const_spec((1, d)), _const_spec(w.shape), tab_spec, tab_spec,
                  _const_spec(pool_w.shape), _const_spec((1, width))],
        out_specs=[out_spec] * 4,
        out_shape=[jax.ShapeDtypeStruct((rows, width), BF16)] * 4,
        compiler_params=_params(("parallel",)),
        name="even_proj_pool",
    )(x2, x2, g, w, cos, sin, pool_w, pool_scale)


def _attn_kernel(lam_ref, q_ref, k_ref, v_ref, sg_ref, o_ref, s_buf, *, lambda_init):
    t = s_buf.shape[2]
    tiles = q_ref.shape[0] // t
    i = pl.program_id(2)
    lane = lax.broadcasted_iota(jnp.int32, (t, GROUP), 1)

    def lane_fold(op, run, s):
        for c in range(t // GROUP):
            run = op(run, s[:, c * GROUP:(c + 1) * GROUP])
        return run

    def tile(nblk, sub):
        q = q_ref[sub * t:(sub + 1) * t, :]
        zero = jnp.zeros_like(q)
        qs = jnp.concatenate([jnp.where(lane < HEAD, q, zero), jnp.where(lane < HEAD, zero, q)], axis=0)
        run = jnp.full((2 * t, GROUP), NEG, F32)
        for j in range(nblk):
            kb = k_ref[j * t:(j + 1) * t, :]
            s = jnp.concatenate([_dot_nt(qs[:t], kb), _dot_nt(qs[t:], kb)], axis=0)
            if j == nblk - 1:
                row = lax.broadcasted_iota(jnp.int32, s.shape, 0)
                col = lax.broadcasted_iota(jnp.int32, s.shape, 1)
                s = jnp.where((col // CHUNK) <= ((row % t) // CHUNK), s, NEG)
            s_buf[j] = s
            run = lane_fold(jnp.maximum, run, s)
        m = jnp.max(run, axis=-1, keepdims=True)

        acc = jnp.zeros((2 * t, 2 * GROUP), F32)
        ones = jnp.ones((t, GROUP), BF16)
        for j in range(nblk):
            pb = jnp.exp2((s_buf[j] - m).astype(BF16))
            v1 = jnp.concatenate([v_ref[j * t:(j + 1) * t, :], ones], axis=1)
            acc = acc + jnp.concatenate([_dot(pb[:t], v1), _dot(pb[t:], v1)], axis=0)
        o = acc[:, :GROUP] / acc[:, GROUP:]
        lv = lam_ref[...]
        lam = (jnp.exp(jnp.sum(lv[0:1] * lv[1:2], axis=-1, keepdims=True))
               - jnp.exp(jnp.sum(lv[2:3] * lv[3:4], axis=-1, keepdims=True)) + lambda_init)
        o = o[:t] - lam * o[t:]
        o = _rms(o, sg_ref[...]) * (1.0 - lambda_init)
        o_ref[sub * t:(sub + 1) * t, :] = o.astype(BF16)

    def step(c):
        for sub in range(tiles):
            tile(c * tiles + sub + 1, sub)

    for c in range(s_buf.shape[0] // tiles):
        pl.when(i == c)(functools.partial(step, c))


def _diff_attention(q, k, v, lam_vecs, subln_g, batch, seq, lambda_init, t):
    rows, width = q.shape
    heads = width // GROUP
    nq = seq // t
    tiles = 2 if nq % 2 == 0 else 1
    steps = nq // tiles
    q_spec = pl.BlockSpec((tiles * t, GROUP), lambda b, h, i: (b * steps + i, h))
    kv_spec = pl.BlockSpec((seq, GROUP), lambda b, h, i: (b, h))
    return pl.pallas_call(
        functools.partial(_attn_kernel, lambda_init=lambda_init),
        grid=(batch, heads, steps),
        in_specs=[_const_spec(lam_vecs.shape), q_spec, kv_spec, kv_spec, _const_spec((1, GROUP))],
        out_specs=q_spec,
        out_shape=jax.ShapeDtypeStruct((rows, width), BF16),
        scratch_shapes=[pltpu.VMEM((nq, 2 * t, t), F32)],
        compiler_params=_params(("parallel", "parallel", "arbitrary")),
        name="diff_attention",
    )(lam_vecs, q, k, v, subln_g)


def _ffn_kernel(*refs, n_acts, tiles_per_seq, tf, final_norm):
    x_ref, halo_ref = refs[0], refs[1]
    act_refs, act_halo_refs = refs[2:2 + n_acts], refs[2 + n_acts:2 + 2 * n_acts]
    wm_ref, g_ref, wi_ref, cw_ref, cb_ref, wo_ref, fg_ref, o_ref, abuf, hbuf = refs[2 + 2 * n_acts:]
    tm = x_ref.shape[0]
    d_ff = hbuf.shape[1]

    def mixed(res, acts):
        row0 = 0
        for a in acts:
            res = res + _dot(a, wm_ref[row0:row0 + a.shape[1], :])[-res.shape[0]:]
            row0 += a.shape[1]
        return res

    x = mixed(x_ref[...], [a[...] for a in act_refs])
    g = g_ref[...]
    xn = _rms(x, g).astype(BF16)
    seq_start = (pl.program_id(0) % tiles_per_seq) == 0
    x_halo = mixed(halo_ref[...], [a[...] for a in act_halo_refs])
    xh = jnp.where(seq_start, 0.0, _rms(x_halo, g)).astype(BF16)

    for c in range(d_ff // tf):
        cols = slice(c * tf, (c + 1) * tf)
        slot = c % 2
        wa = wi_ref[:, cols]
        abuf[slot, 0:HALO, :] = _dot(xh, wa)
        a = _dot(xn, wa)
        abuf[slot, HALO:, :] = a
        gate = _dot(xn, wi_ref[:, d_ff + c * tf:d_ff + (c + 1) * tf])
        cw = cw_ref[:, cols]
        conv = (abuf[slot, HALO - 2:HALO - 2 + tm, :] * cw[0:1]
                + abuf[slot, HALO - 1:HALO - 1 + tm, :] * cw[1:2] + a * cw[2:3] + cb_ref[:, cols])
        act = 0.5 * conv * (1.0 + lax.erf(conv * (2.0 ** -0.5)))
        hbuf[:, cols] = (act * gate).astype(BF16)
    out = x + _dot(hbuf[...], wo_ref[...])
    if final_norm:
        out = _rms(out, fg_ref[...])
    o_ref[...] = out


def _ffn(x2, acts, w_mix, g, w_in, conv_w, conv_b, w_out, final_g, layer, seq, tm, tf, final_norm):
    rows, d = x2.shape
    d_ff = w_out.shape[1]
    tiles_per_seq = seq // tm
    row_spec = pl.BlockSpec((tm, d), lambda i: (i, 0))
    halo_spec = pl.BlockSpec((HALO, d), lambda i: (jnp.maximum(i * (tm // HALO) - 1, 0), 0))
    act_specs = [pl.BlockSpec((tm, a.shape[1]), lambda i: (i, 0)) for a in acts]
    act_halo_specs = [pl.BlockSpec((2 * HALO, a.shape[1]),
                                   lambda i: (jnp.maximum(i * (tm // (2 * HALO)) - 1, 0), 0)) for a in acts]
    return pl.pallas_call(
        functools.partial(_ffn_kernel, n_acts=len(acts), tiles_per_seq=tiles_per_seq, tf=tf,
                          final_norm=final_norm),
        grid=(rows // tm,),
        in_specs=[row_spec, halo_spec] + act_specs + act_halo_specs
        + [_const_spec(w_mix.shape), _const_spec((1, d)), _layer_spec(w_in.shape, layer),
           _const_spec(conv_w.shape), _const_spec((1, d_ff)), _layer_spec(w_out.shape, layer),
           _const_spec((1, d))],
        out_specs=row_spec,
        out_shape=jax.ShapeDtypeStruct((rows, d), F32),
        scratch_shapes=[pltpu.VMEM((2, HALO + tm, tf), F32), pltpu.VMEM((tm, d_ff), BF16)],
        compiler_params=_params(("parallel",)),
        name="mix_proj_conv_ffn",
    )(x2, x2, *acts, *acts, w_mix, g, w_in, conv_w, conv_b, w_out, final_g)


def _rwkv_proj_kernel(x_ref, halo_ref, g_ref, mu_ref, wr_ref, wk_ref, wv_ref, w0_ref, w1_ref,
                      w2_ref, a0_ref, a1_ref, a2_ref, g1_ref, g2_ref, kk_ref, ka_ref,
                      r_out, ld_out, k_out, v_out, kk_out, a_out, g_out, *, tiles_per_seq, tn):
    g = g_ref[...]
    h = _rms(x_ref[...], g)
    seq_start = (pl.program_id(0) % tiles_per_seq) == 0
    h_halo = jnp.where(seq_start, 0.0, _rms(halo_ref[...], g))
    xx = _shift_rows(h, h_halo, 1) - h
    hb, xb, mu = h.astype(BF16), xx.astype(BF16), mu_ref[...].astype(BF16)

    def mix(idx):
        return hb + xb * mu[idx:idx + 1, :]

    xr, xk, xv = mix(0), mix(2), mix(3)
    w_hid = jnp.tanh(_dot(mix(1), w1_ref[...])).astype(BF16)
    a_hid = _dot(mix(4), a1_ref[...]).astype(BF16)
    g_hid = jax.nn.sigmoid(_dot(mix(5), g1_ref[...])).astype(BF16)
    for c in range(h.shape[1] // tn):
        cols = slice(c * tn, (c + 1) * tn)
        r_out[:, cols] = _dot(xr, wr_ref[:, cols]).astype(BF16)
        w_lin = w0_ref[:, cols] + _dot(w_hid, w2_ref[:, cols])
        w_log = -jax.nn.softplus(-w_lin) - 0.5
        ld_out[:, cols] = -jnp.exp(w_log)
        k = _dot(xk, wk_ref[:, cols])
        v_out[:, cols] = _dot(xv, wv_ref[:, cols]).astype(BF16)
        a = jax.nn.sigmoid(a0_ref[:, cols] + _dot(a_hid, a2_ref[:, cols]))
        a_out[:, cols] = a.astype(BF16)
        g_out[:, cols] = _dot(g_hid, g2_ref[:, cols]).astype(BF16)
        kk_out[:, cols] = (k * kk_ref[:, cols]).astype(BF16)
        k_out[:, cols] = (k * (1.0 + (a - 1.0) * ka_ref[:, cols])).astype(BF16)


def _rwkv_proj(x2, g, mu, wr, wk, wv, w0, w1, w2, a0, a1, a2, g1, g2, k_k, k_a, seq, tm):
    rows, d = x2.shape
    tiles_per_seq = seq // tm
    halo_blocks = tm // HALO
    row_spec = pl.BlockSpec((tm, d), lambda i: (i, 0))
    halo_spec = pl.BlockSpec((HALO, d), lambda i: (jnp.maximum(i * halo_blocks - 1, 0), 0))
    vec = _const_spec((1, d))
    consts = [mu, wr, wk, wv, w0, w1, w2, a0, a1, a2, g1, g2, k_k, k_a]
    return pl.pallas_call(
        functools.partial(_rwkv_proj_kernel, tiles_per_seq=tiles_per_seq, tn=256),
        grid=(rows // tm,),
        in_specs=[row_spec, halo_spec, vec] + [_const_spec(c.shape) for c in consts],
        out_specs=[row_spec] * 7,
        out_shape=[jax.ShapeDtypeStruct((rows, d), F32 if i == 1 else BF16) for i in range(7)],
        compiler_params=_params(("parallel",)),
        name="rwkv_proj",
    )(x2, x2, g, *consts)


def _rwkv_rec_kernel(r_ref, ld_ref, k_ref, v_ref, kk_ref, a_ref, g_ref, rk_ref, lw_ref, lb_ref,
                     o_ref, state):
    L = RWKV_L
    seqs = r_ref.shape[0]
    chunks = r_ref.shape[1] // L
    pairs = r_ref.shape[2] // GROUP
    lanes = seqs * pairs

    @pl.when(pl.program_id(2) == 0)
    def _():
        state[...] = jnp.zeros(state.shape, F32)

    def stack(t):
        return jnp.stack([t[q, c * L:(c + 1) * L, p * GROUP:(p + 1) * GROUP]
                          for c in range(chunks) for q in range(seqs) for p in range(pairs)], axis=0)

    def per_lane(ref, reps):
        return jnp.stack([ref[:, p * GROUP:(p + 1) * GROUP] for p in range(pairs)] * (reps * seqs), axis=0)

    head0 = lax.broadcasted_iota(jnp.int32, (1, L, GROUP), 2) < HEAD

    def split(t):
        return jnp.concatenate([jnp.where(head0, t, 0.0), jnp.where(head0, 0.0, t)], axis=1)

    ones_r = lax.broadcasted_iota(jnp.int32, (GROUP, GROUP), 0) // HEAD
    ones_c = lax.broadcasted_iota(jnp.int32, (GROUP, GROUP), 1) // HEAD
    head_ones = (ones_r == ones_c).astype(BF16)

    def head_sum(t):
        n = t.shape[0]
        return _dot(t.reshape(n * L, GROUP).astype(BF16), head_ones).reshape(n, L, GROUP)

    row = lax.broadcasted_iota(jnp.int32, (1, 4 * L, 2 * L), 1)
    col = lax.broadcasted_iota(jnp.int32, (1, 4 * L, 2 * L), 2)
    same_head = ((row // L) % 2) == (col // L)
    strict = (col % L) < (row % L)
    keep = same_head & (strict | ((row >= 2 * L) & ((col % L) == (row % L))))
    r2 = lax.broadcasted_iota(jnp.int32, (1, 2 * L, 2 * L), 1)
    c2 = lax.broadcasted_iota(jnp.int32, (1, 2 * L, 2 * L), 2)
    eye = (r2 == c2).astype(F32)
    block_diag = (r2 // HEAD) == (c2 // HEAD)
    tl = lax.broadcasted_iota(jnp.int32, (chunks * L, chunks * L), 0)
    tc = lax.broadcasted_iota(jnp.int32, (chunks * L, chunks * L), 1)
    tri = ((tc <= tl) & ((tc // L) == (tl // L))).astype(BF16)

    lnx_w, lnx_b, r_k = per_lane(lw_ref, 1), per_lane(lb_ref, 1), per_lane(rk_ref, chunks)

    def load(ref):
        return stack(ref[...]).astype(F32)

    ld3 = ld_ref[...]
    width = ld3.shape[2]
    ld_hi = ld3.astype(BF16)
    ld_lo = (ld3 - ld_hi.astype(F32)).astype(BF16)
    cum = []
    for q in range(seqs):
        c2x = _dot(tri, jnp.concatenate([ld_hi[q], ld_lo[q]], axis=1))
        cum.append(c2x[:, :width] + c2x[:, width:])
    cum = stack(jnp.stack(cum, axis=0))
    ld2 = ld3

    r, ld, kx, v, kk, a = load(r_ref), stack(ld2), load(k_ref), load(v_ref), load(kk_ref), load(a_ref)
    kkn = kk * jnp.minimum(lax.rsqrt(head_sum(kk * kk)), 1e12)
    p_in = jnp.exp(cum)
    inv_p = jnp.exp(-cum)
    rt = r * p_in
    at = -kkn * jnp.exp(cum - ld)
    bt = (kkn * a * inv_p).astype(BF16)
    kt = (kx * inv_p).astype(BF16)

    lhs = jnp.concatenate([at, rt], axis=1).astype(BF16)
    xbk = _bdot_nt(lhs, jnp.concatenate([split(bt), split(kt)], axis=1))
    xa, xr = xbk[:, :L], xbk[:, L:]
    xb = jnp.where(keep, jnp.concatenate([xa[:, :, :2 * L]] * 2 + [xr[:, :, :2 * L]] * 2, axis=1), 0.0)
    xk = jnp.where(keep, jnp.concatenate([xa[:, :, 2 * L:]] * 2 + [xr[:, :, 2 * L:]] * 2, axis=1), 0.0)
    m_ab = xb[:, :2 * L]
    m_r = jnp.concatenate([xb[:, 2 * L:], xk[:, 2 * L:]], axis=2).astype(BF16)
    bk = jnp.concatenate([bt, kt], axis=1)
    v_st = split(v).astype(BF16)
    p_last = p_in[:, L - 1:L, :]
    bonus = head_sum(r * kx * r_k) * v
    gate = load(g_ref)

    n = _bdot(m_ab.astype(BF16), m_ab.astype(BF16))
    t_inv = eye + m_ab
    levels = int(math.log2(L))
    for level in range(1, levels):
        nb = n.astype(BF16)
        if level < levels - 1:
            prod = _bdot(nb, jnp.concatenate([nb, t_inv.astype(BF16)], axis=2))
            n, t_inv = prod[:, :, :2 * L], t_inv + prod[:, :, 2 * L:]
        else:
            t_inv = t_inv + _bdot(nb, t_inv.astype(BF16))
    t_inv = t_inv.astype(BF16)
    akv = _bdot(xk[:, :2 * L].astype(BF16), v_st)

    s_bd = state[...]
    for c in range(chunks):
        u = slice(c * lanes, (c + 1) * lanes)
        xs = _bdot_nt(lhs[u], s_bd.astype(BF16))
        sa_st = _bdot(t_inv[u], (split(xs[:, :L]) + akv[u]).astype(BF16))
        y_st = _bdot(m_r[u], jnp.concatenate([sa_st.astype(BF16), v_st[u]], axis=1))
        y = xs[:, L:] + y_st[:, :L] + y_st[:, L:]
        sa = sa_st[:, :L] + sa_st[:, L:]
        ds = _bdot_tn(jnp.concatenate([sa, v[u]], axis=1).astype(BF16), bk[u])
        s_bd = jnp.where(block_diag, s_bd + ds, 0.0) * p_last[u]

        mean = head_sum(y) * (1.0 / HEAD)
        dev = y - mean
        var = head_sum(dev * dev) * (1.0 / HEAD)
        yn = dev * lax.rsqrt(var + RWKV_GN_EPS) * lnx_w + lnx_b
        out = ((yn + bonus[u]) * gate[u]).astype(BF16)
        for q in range(seqs):
            for p in range(pairs):
                o_ref[q, c * L:(c + 1) * L, p * GROUP:(p + 1) * GROUP] = out[q * pairs + p]
    state[...] = s_bd


def _rwkv_rec(r, ld, k, v, kk, a, g, r_k, lnx_w, lnx_b, batch, seq, pairs, chunks, seqs):
    rows, d = r.shape
    width = pairs * GROUP
    groups = d // width
    nc = seq // (chunks * RWKV_L)
    spec = pl.BlockSpec((seqs, chunks * RWKV_L, width), lambda b, p, c: (b, c, p))
    vec = pl.BlockSpec((1, width), lambda b, p, c: (0, p))
    per_seq = [t.reshape(batch, seq, d) for t in (r, ld, k, v, kk, a, g)]
    out = pl.pallas_call(
        _rwkv_rec_kernel,
        grid=(batch // seqs, groups, nc),
        in_specs=[spec] * 7 + [vec] * 3,
        out_specs=spec,
        out_shape=jax.ShapeDtypeStruct((batch, seq, d), BF16),
        scratch_shapes=[pltpu.VMEM((seqs * pairs, GROUP, GROUP), F32)],
        compiler_params=_params(("parallel", "parallel", "arbitrary")),
        name="rwkv_recurrence",
    )(*per_seq, r_k, lnx_w, lnx_b)
    return out.reshape(rows, d)


def _rotary_tables(seq):
    inv_freq = 1.0 / (ROPE_THETA ** (jnp.arange(0, HEAD, 2, dtype=F32) / HEAD))
    ang = jnp.arange(seq, dtype=F32)[:, None] * inv_freq[None, :]
    cos, sin = jnp.cos(ang), jnp.sin(ang)
    cos = jnp.concatenate([cos, cos, cos, cos], axis=1)
    sin = jnp.concatenate([-sin, sin, -sin, sin], axis=1)
    return cos, sin


def _pad_cols(w):
    return jnp.pad(w, ((0, 0), (0, LORA_PAD - w.shape[1])))


def _pad_rows(w):
    return jnp.pad(w, ((0, LORA_PAD - w.shape[0]), (0, 0)))


def kernel(x, e_norm_g, e_w_in, e_pool_w, e_pool_scale, e_lambda, e_subln_g, e_w_o, o_norm_g, o_mu, o_w_r, o_w_k, o_w_v, o_w_o, o_w0, o_w1, o_w2, o_a0, o_a1, o_a2, o_g1, o_g2, o_k_k, o_k_a, o_r_k, o_lnx_w, o_lnx_b, f_norm_g, f_w_in, f_conv_w, f_conv_b, f_w_out, final_g):
    batch, seq, d = x.shape
    depth = f_norm_g.shape[0]
    tm = min(512, seq)
    attn_t = min(512, seq)
    x2 = x.reshape(batch * seq, d)
    row = lambda vec: vec.reshape(1, -1)
    cos, sin = _rotary_tables(seq)

    f_w_in_b, f_w_out_b = f_w_in.astype(BF16), f_w_out.astype(BF16)

    for layer in range(depth):
        i = layer // 2
        if layer % 2 == 0:
            lambda_init = 0.8 - 0.6 * math.exp(-0.3 * layer)
            pool_out, q, k, v = _even_proj(x2, row(e_norm_g[i]), e_w_in[i].astype(BF16), cos, sin,
                                           e_pool_w[i].astype(BF16), row(e_pool_scale[i]), seq, tm)
            attn_out = _diff_attention(q, k, v, e_lambda[i], row(e_subln_g[i]), batch, seq,
                                       lambda_init, attn_t)
            acts, w_mix = [pool_out, attn_out], e_w_o[i].astype(BF16)
        else:
            r, ld, k, v, kk, a, g = _rwkv_proj(
                x2, row(o_norm_g[i]), o_mu[i], o_w_r[i].astype(BF16), o_w_k[i].astype(BF16),
                o_w_v[i].astype(BF16), row(o_w0[i]), _pad_cols(o_w1[i]).astype(BF16),
                _pad_rows(o_w2[i]).astype(BF16), row(o_a0[i]), _pad_cols(o_a1[i]).astype(BF16),
                _pad_rows(o_a2[i]).astype(BF16), o_g1[i].astype(BF16), o_g2[i].astype(BF16),
                row(o_k_k[i]), row(o_k_a[i]), seq, tm)
            z = _rwkv_rec(r, ld, k, v, kk, a, g, row(o_r_k[i]), row(o_lnx_w[i]), row(o_lnx_b[i]),
                          batch, seq, pairs=8, chunks=4, seqs=2 if batch % 2 == 0 else 1)
            acts, w_mix = [z], o_w_o[i].astype(BF16)
        x2 = _ffn(x2, acts, w_mix, row(f_norm_g[layer]), f_w_in_b, f_conv_w[layer],
                  row(f_conv_b[layer]), f_w_out_b, row(final_g), layer, seq,
                  min(1024, seq), tf=256, final_norm=(layer == depth - 1))
    return x2.reshape(batch, seq, d)
```

```python
import functools
import math

import jax
import jax.numpy as jnp
from jax import lax
from jax.experimental import pallas as pl
from jax.experimental.pallas import tpu as pltpu

F32 = jnp.float32
BF16 = jnp.bfloat16

RMS_EPS = 1e-5
ROPE_THETA = 10000.0
CHUNK = 64
POOL_WINDOWS = (2, 4, 8, 16)
POOL_HALO = 16
GROUP = 128
HEAD = 64
RWKV_GN_EPS = 64e-5
RWKV_L = 64
LORA_PAD = 128
HALO = 8
NEG = -1e30
LOG2_E = 1.4426950408889634
VMEM_LIMIT = 56 * 1024 * 1024


def _rms(x, g):
    return x * lax.rsqrt(jnp.mean(x * x, axis=-1, keepdims=True) + RMS_EPS) * g


def _dot(a, b):
    return jnp.dot(a, b, preferred_element_type=F32)


def _dot_nt(a, b):
    return lax.dot_general(a, b, (((1,), (1,)), ((), ())), preferred_element_type=F32)


def _dot_tn(a, b):
    return lax.dot_general(a, b, (((0,), (0,)), ((), ())), preferred_element_type=F32)


def _bdot(a, b):
    return lax.dot_general(a, b, (((2,), (1,)), ((0,), (0,))), preferred_element_type=F32)


def _bdot_nt(a, b):
    return lax.dot_general(a, b, (((2,), (2,)), ((0,), (0,))), preferred_element_type=F32)


def _bdot_tn(a, b):
    return lax.dot_general(a, b, (((1,), (1,)), ((0,), (0,))), preferred_element_type=F32)


def _shift_rows(t, halo, n):
    rolled = pltpu.roll(t, n, 0)
    row = lax.broadcasted_iota(jnp.int32, halo.shape, 0)
    head = jnp.where(row < n, pltpu.roll(halo, n, 0), rolled[:HALO])
    return jnp.concatenate([head, rolled[HALO:]], axis=0)


def _const_spec(shape):
    return pl.BlockSpec(shape, lambda *_: (0,) * len(shape), pipeline_mode=pl.Buffered(1))


def _layer_spec(shape, layer):
    return pl.BlockSpec((None,) + tuple(shape[1:]), lambda *_: (layer,) + (0,) * (len(shape) - 1),
                        pipeline_mode=pl.Buffered(1))


def _params(sem):
    return pltpu.CompilerParams(dimension_semantics=sem, vmem_limit_bytes=VMEM_LIMIT)


def _even_proj_kernel(x_ref, halo_ref, g_ref, w_ref, cos_ref, sin_ref, pw_ref, ps_ref,
                      pool_ref, q_ref, k_ref, v_ref, *, tiles_per_seq):
    tm, width = q_ref.shape
    g = g_ref[...]
    xn = _rms(x_ref[...], g).astype(BF16)
    cos = jnp.concatenate([cos_ref[...]] * (width // GROUP), axis=1)
    sin = jnp.concatenate([sin_ref[...]] * (width // GROUP), axis=1)
    lane = lax.broadcasted_iota(jnp.int32, cos.shape, 1)
    first_half = (lane % HEAD) < HEAD // 2

    def proj(idx):
        return _dot(xn, w_ref[:, idx * width:(idx + 1) * width])

    def rope(t):
        partner = jnp.where(first_half, pltpu.roll(t, width - HEAD // 2, 1),
                            pltpu.roll(t, HEAD // 2, 1))
        return t * cos + partner * sin

    q_ref[...] = (rope(proj(1)) * (HEAD ** -0.5 * LOG2_E)).astype(BF16)
    k_ref[...] = rope(proj(2)).astype(BF16)
    v_ref[...] = proj(3).astype(BF16)

    tile_idx = pl.program_id(0) % tiles_per_seq
    u = proj(0)
    xh = jnp.where(tile_idx == 0, 0.0, _rms(halo_ref[...], g)).astype(BF16)
    ext = jnp.concatenate([_dot(xh, w_ref[:, :width]), u], axis=0)
    pos = tile_idx * tm + lax.broadcasted_iota(jnp.int32, (tm, 1), 0)
    for gi, win in enumerate(POOL_WINDOWS):
        sl = slice(gi * GROUP, (gi + 1) * GROUP)
        acc, span = ext[:, sl], 1
        while span < win:
            acc = acc + pltpu.roll(acc, span, 0)
            span *= 2
        inv_count = 1.0 / jnp.minimum(pos + 1, win).astype(F32)
        delta = acc[POOL_HALO:] * inv_count - u[:, sl]
        y = _dot(delta.astype(BF16), pw_ref[gi]) * ps_ref[:, sl]
        pool_ref[:, sl] = y.astype(BF16)


def _even_proj(x2, g, w, cos, sin, pool_w, pool_scale, seq, tm):
    rows, d = x2.shape
    width = w.shape[1] // 4
    tiles_per_seq = seq // tm
    halo_blocks = tm // POOL_HALO
    row_spec = pl.BlockSpec((tm, d), lambda i: (i, 0))
    halo_spec = pl.BlockSpec((POOL_HALO, d), lambda i: (jnp.maximum(i * halo_blocks - 1, 0), 0))
    out_spec = pl.BlockSpec((tm, width), lambda i: (i, 0))
    tab_spec = pl.BlockSpec((tm, GROUP), lambda i: (i % tiles_per_seq, 0))
    return pl.pallas_call(
        functools.partial(_even_proj_kernel, tiles_per_seq=tiles_per_seq),
        grid=(rows // tm,),
        in_specs=[row_spec, halo_spec, _const_spec((1, d)), _const_spec(w.shape), tab_spec, tab_spec,
                  _const_spec(pool_w.shape), _const_spec((1, width))],
        out_specs=[out_spec] * 4,
        out_shape=[jax.ShapeDtypeStruct((rows, width), BF16)] * 4,
        compiler_params=_params(("parallel",)),
        name="even_proj_pool",
    )(x2, x2, g, w, cos, sin, pool_w, pool_scale)


def _attn_kernel(lam_ref, q_ref, k_ref, v_ref, sg_ref, o_ref, s_buf, *, lambda_init):
    t = s_buf.shape[2]
    tiles = q_ref.shape[0] // t
    i = pl.program_id(2)
    lane = lax.broadcasted_iota(jnp.int32, (t, GROUP), 1)

    def lane_fold(op, run, s):
        for c in range(t // GROUP):
            run = op(run, s[:, c * GROUP:(c + 1) * GROUP])
        return run

    def tile(nblk, sub):
        q = q_ref[sub * t:(sub + 1) * t, :]
        zero = jnp.zeros_like(q)
        qs = jnp.concatenate([jnp.where(lane < HEAD, q, zero), jnp.where(lane < HEAD, zero, q)], axis=0)
        run = jnp.full((2 * t, GROUP), NEG, F32)
        for j in range(nblk):
            kb = k_ref[j * t:(j + 1) * t, :]
            s = jnp.concatenate([_dot_nt(qs[:t], kb), _dot_nt(qs[t:], kb)], axis=0)
            if j == nblk - 1:
                row = lax.broadcasted_iota(jnp.int32, s.shape, 0)
                col = lax.broadcasted_iota(jnp.int32, s.shape, 1)
                s = jnp.where((col // CHUNK) <= ((row % t) // CHUNK), s, NEG)
            s_buf[j] = s
            run = lane_fold(jnp.maximum, run, s)
        m = jnp.max(run, axis=-1, keepdims=True)

        acc = jnp.zeros((2 * t, 2 * GROUP), F32)
        ones = jnp.ones((t, GROUP), BF16)
        for j in range(nblk):
            pb = jnp.exp2((s_buf[j] - m).astype(BF16))
            v1 = jnp.concatenate([v_ref[j * t:(j + 1) * t, :], ones], axis=1)
            acc = acc + jnp.concatenate([_dot(pb[:t], v1), _dot(pb[t:], v1)], axis=0)
        o = acc[:, :GROUP] / acc[:, GROUP:]
        lv = lam_ref[...]
        lam = (jnp.exp(jnp.sum(lv[0:1] * lv[1:2], axis=-1, keepdims=True))
               - jnp.exp(jnp.sum(lv[2:3] * lv[3:4], axis=-1, keepdims=True)) + lambda_init)
        o = o[:t] - lam * o[t:]
        o = _rms(o, sg_ref[...]) * (1.0 - lambda_init)
        o_ref[sub * t:(sub + 1) * t, :] = o.astype(BF16)

    def step(c):
        for sub in range(tiles):
            tile(c * tiles + sub + 1, sub)

    for c in range(s_buf.shape[0] // tiles):
        pl.when(i == c)(functools.partial(step, c))


def _diff_attention(q, k, v, lam_vecs, subln_g, batch, seq, lambda_init, t):
    rows, width = q.shape
    heads = width // GROUP
    nq = seq // t
    tiles = 4 if nq % 4 == 0 else (2 if nq % 2 == 0 else 1)
    steps = nq // tiles
    q_spec = pl.BlockSpec((tiles * t, GROUP), lambda b, h, i: (b * steps + i, h))
    kv_spec = pl.BlockSpec((seq, GROUP), lambda b, h, i: (b, h))
    return pl.pallas_call(
        functools.partial(_attn_kernel, lambda_init=lambda_init),
        grid=(batch, heads, steps),
        in_specs=[_const_spec(lam_vecs.shape), q_spec, kv_spec, kv_spec, _const_spec((1, GROUP))],
        out_specs=q_spec,
        out_shape=jax.ShapeDtypeStruct((rows, width), BF16),
        scratch_shapes=[pltpu.VMEM((nq, 2 * t, t), F32)],
        compiler_params=_params(("parallel", "parallel", "arbitrary")),
        name="diff_attention",
    )(lam_vecs, q, k, v, subln_g)


def _ffn_kernel(*refs, n_acts, tiles_per_seq, tf, final_norm):
    x_ref, halo_ref = refs[0], refs[1]
    act_refs, act_halo_refs = refs[2:2 + n_acts], refs[2 + n_acts:2 + 2 * n_acts]
    wm_ref, g_ref, wi_ref, cw_ref, cb_ref, wo_ref, fg_ref, o_ref, abuf, hbuf = refs[2 + 2 * n_acts:]
    tm = x_ref.shape[0]
    d_ff = hbuf.shape[1]

    def mixed(res, acts):
        row0 = 0
        for a in acts:
            res = res + _dot(a, wm_ref[row0:row0 + a.shape[1], :])[-res.shape[0]:]
            row0 += a.shape[1]
        return res

    x = mixed(x_ref[...], [a[...] for a in act_refs])
    g = g_ref[...]
    xn = _rms(x, g).astype(BF16)
    seq_start = (pl.program_id(0) % tiles_per_seq) == 0
    x_halo = mixed(halo_ref[...], [a[...] for a in act_halo_refs])
    xh = jnp.where(seq_start, 0.0, _rms(x_halo, g)).astype(BF16)

    for c in range(d_ff // tf):
        cols = slice(c * tf, (c + 1) * tf)
        slot = c % 2
        wa = wi_ref[:, cols]
        abuf[slot, 0:HALO, :] = _dot(xh, wa)
        a = _dot(xn, wa)
        abuf[slot, HALO:, :] = a
        gate = _dot(xn, wi_ref[:, d_ff + c * tf:d_ff + (c + 1) * tf])
        cw = cw_ref[:, cols]
        conv = (abuf[slot, HALO - 2:HALO - 2 + tm, :] * cw[0:1]
                + abuf[slot, HALO - 1:HALO - 1 + tm, :] * cw[1:2] + a * cw[2:3] + cb_ref[:, cols])
        act = 0.5 * conv * (1.0 + lax.erf(conv * (2.0 ** -0.5)))
        hbuf[:, cols] = (act * gate).astype(BF16)
    out = x + _dot(hbuf[...], wo_ref[...])
    if final_norm:
        out = _rms(out, fg_ref[...])
    o_ref[...] = out


def _ffn(x2, acts, w_mix, g, w_in, conv_w, conv_b, w_out, final_g, layer, seq, tm, tf, final_norm):
    rows, d = x2.shape
    d_ff = w_out.shape[1]
    tiles_per_seq = seq // tm
    row_spec = pl.BlockSpec((tm, d), lambda i: (i, 0))
    halo_spec = pl.BlockSpec((HALO, d), lambda i: (jnp.maximum(i * (tm // HALO) - 1, 0), 0))
    act_specs = [pl.BlockSpec((tm, a.shape[1]), lambda i: (i, 0)) for a in acts]
    act_halo_specs = [pl.BlockSpec((2 * HALO, a.shape[1]),
                                   lambda i: (jnp.maximum(i * (tm // (2 * HALO)) - 1, 0), 0)) for a in acts]
    return pl.pallas_call(
        functools.partial(_ffn_kernel, n_acts=len(acts), tiles_per_seq=tiles_per_seq, tf=tf,
                          final_norm=final_norm),
        grid=(rows // tm,),
        in_specs=[row_spec, halo_spec] + act_specs + act_halo_specs
        + [_const_spec(w_mix.shape), _const_spec((1, d)), _layer_spec(w_in.shape, layer),
           _const_spec(conv_w.shape), _const_spec((1, d_ff)), _layer_spec(w_out.shape, layer),
           _const_spec((1, d))],
        out_specs=row_spec,
        out_shape=jax.ShapeDtypeStruct((rows, d), F32),
        scratch_shapes=[pltpu.VMEM((2, HALO + tm, tf), F32), pltpu.VMEM((tm, d_ff), BF16)],
        compiler_params=_params(("parallel",)),
        name="mix_proj_conv_ffn",
    )(x2, x2, *acts, *acts, w_mix, g, w_in, conv_w, conv_b, w_out, final_g)


def _rwkv_proj_kernel(x_ref, halo_ref, g_ref, mu_ref, wr_ref, wk_ref, wv_ref, w0_ref, w1_ref,
                      w2_ref, a0_ref, a1_ref, a2_ref, g1_ref, g2_ref, kk_ref, ka_ref,
                      r_out, ld_out, k_out, v_out, kk_out, a_out, g_out, *, tiles_per_seq, tn):
    g = g_ref[...]
    h = _rms(x_ref[...], g)
    seq_start = (pl.program_id(0) % tiles_per_seq) == 0
    h_halo = jnp.where(seq_start, 0.0, _rms(halo_ref[...], g))
    xx = _shift_rows(h, h_halo, 1) - h
    hb, xb, mu = h.astype(BF16), xx.astype(BF16), mu_ref[...].astype(BF16)

    def mix(idx):
        return hb + xb * mu[idx:idx + 1, :]

    xr, xk, xv = mix(0), mix(2), mix(3)
    w_hid = jnp.tanh(_dot(mix(1), w1_ref[...])).astype(BF16)
    a_hid = _dot(mix(4), a1_ref[...]).astype(BF16)
    g_hid = jax.nn.sigmoid(_dot(mix(5), g1_ref[...])).astype(BF16)
    for c in range(h.shape[1] // tn):
        cols = slice(c * tn, (c + 1) * tn)
        r_out[:, cols] = _dot(xr, wr_ref[:, cols]).astype(BF16)
        w_lin = w0_ref[:, cols] + _dot(w_hid, w2_ref[:, cols])
        w_log = -jax.nn.softplus(-w_lin) - 0.5
        ld_out[:, cols] = -jnp.exp(w_log)
        k = _dot(xk, wk_ref[:, cols])
        v_out[:, cols] = _dot(xv, wv_ref[:, cols]).astype(BF16)
        a = jax.nn.sigmoid(a0_ref[:, cols] + _dot(a_hid, a2_ref[:, cols]))
        a_out[:, cols] = a.astype(BF16)
        g_out[:, cols] = _dot(g_hid, g2_ref[:, cols]).astype(BF16)
        kk_out[:, cols] = (k * kk_ref[:, cols]).astype(BF16)
        k_out[:, cols] = (k * (1.0 + (a - 1.0) * ka_ref[:, cols])).astype(BF16)


def _rwkv_proj(x2, g, mu, wr, wk, wv, w0, w1, w2, a0, a1, a2, g1, g2, k_k, k_a, seq, tm):
    rows, d = x2.shape
    tiles_per_seq = seq // tm
    halo_blocks = tm // HALO
    row_spec = pl.BlockSpec((tm, d), lambda i: (i, 0))
    halo_spec = pl.BlockSpec((HALO, d), lambda i: (jnp.maximum(i * halo_blocks - 1, 0), 0))
    vec = _const_spec((1, d))
    consts = [mu, wr, wk, wv, w0, w1, w2, a0, a1, a2, g1, g2, k_k, k_a]
    return pl.pallas_call(
        functools.partial(_rwkv_proj_kernel, tiles_per_seq=tiles_per_seq, tn=256),
        grid=(rows // tm,),
        in_specs=[row_spec, halo_spec, vec] + [_const_spec(c.shape) for c in consts],
        out_specs=[row_spec] * 7,
        out_shape=[jax.ShapeDtypeStruct((rows, d), F32 if i == 1 else BF16) for i in range(7)],
        compiler_params=_params(("parallel",)),
        name="rwkv_proj",
    )(x2, x2, g, *consts)


def _rwkv_rec_kernel(r_ref, ld_ref, k_ref, v_ref, kk_ref, a_ref, g_ref, rk_ref, lw_ref, lb_ref,
                     o_ref, state):
    L = RWKV_L
    seqs = r_ref.shape[0]
    chunks = r_ref.shape[1] // L
    pairs = r_ref.shape[2] // GROUP
    lanes = seqs * pairs

    @pl.when(pl.program_id(2) == 0)
    def _():
        state[...] = jnp.zeros(state.shape, F32)

    def stack(t):
        return jnp.stack([t[q, c * L:(c + 1) * L, p * GROUP:(p + 1) * GROUP]
                          for c in range(chunks) for q in range(seqs) for p in range(pairs)], axis=0)

    def per_lane(ref, reps):
        return jnp.stack([ref[:, p * GROUP:(p + 1) * GROUP] for p in range(pairs)] * (reps * seqs), axis=0)

    head0 = lax.broadcasted_iota(jnp.int32, (1, L, GROUP), 2) < HEAD

    def split(t):
        return jnp.concatenate([jnp.where(head0, t, 0.0), jnp.where(head0, 0.0, t)], axis=1)

    ones_r = lax.broadcasted_iota(jnp.int32, (GROUP, GROUP), 0) // HEAD
    ones_c = lax.broadcasted_iota(jnp.int32, (GROUP, GROUP), 1) // HEAD
    head_ones = (ones_r == ones_c).astype(BF16)

    def head_sum(t):
        n = t.shape[0]
        return _dot(t.reshape(n * L, GROUP).astype(BF16), head_ones).reshape(n, L, GROUP)

    row = lax.broadcasted_iota(jnp.int32, (1, 4 * L, 2 * L), 1)
    col = lax.broadcasted_iota(jnp.int32, (1, 4 * L, 2 * L), 2)
    same_head = ((row // L) % 2) == (col // L)
    strict = (col % L) < (row % L)
    keep = same_head & (strict | ((row >= 2 * L) & ((col % L) == (row % L))))
    r2 = lax.broadcasted_iota(jnp.int32, (1, 2 * L, 2 * L), 1)
    c2 = lax.broadcasted_iota(jnp.int32, (1, 2 * L, 2 * L), 2)
    eye = (r2 == c2).astype(F32)
    block_diag = (r2 // HEAD) == (c2 // HEAD)
    tl = lax.broadcasted_iota(jnp.int32, (chunks * L, chunks * L), 0)
    tc = lax.broadcasted_iota(jnp.int32, (chunks * L, chunks * L), 1)
    tri = ((tc <= tl) & ((tc // L) == (tl // L))).astype(BF16)

    lnx_w, lnx_b, r_k = per_lane(lw_ref, 1), per_lane(lb_ref, 1), per_lane(rk_ref, chunks)

    def load(ref):
        return stack(ref[...]).astype(F32)

    ld3 = ld_ref[...]
    width = ld3.shape[2]
    ld_hi = ld3.astype(BF16)
    ld_lo = (ld3 - ld_hi.astype(F32)).astype(BF16)
    cum = []
    for q in range(seqs):
        c2x = _dot(tri, jnp.concatenate([ld_hi[q], ld_lo[q]], axis=1))
        cum.append(c2x[:, :width] + c2x[:, width:])
    cum = stack(jnp.stack(cum, axis=0))
    ld2 = ld3

    r, ld, kx, v, kk, a = load(r_ref), stack(ld2), load(k_ref), load(v_ref), load(kk_ref), load(a_ref)
    kkn = kk * jnp.minimum(lax.rsqrt(head_sum(kk * kk)), 1e12)
    p_in = jnp.exp(cum)
    inv_p = jnp.exp(-cum)
    rt = r * p_in
    at = -kkn * jnp.exp(cum - ld)
    bt = (kkn * a * inv_p).astype(BF16)
    kt = (kx * inv_p).astype(BF16)

    lhs = jnp.concatenate([at, rt], axis=1).astype(BF16)
    xbk = _bdot_nt(lhs, jnp.concatenate([split(bt), split(kt)], axis=1))
    xa, xr = xbk[:, :L], xbk[:, L:]
    xb = jnp.where(keep, jnp.concatenate([xa[:, :, :2 * L]] * 2 + [xr[:, :, :2 * L]] * 2, axis=1), 0.0)
    xk = jnp.where(keep, jnp.concatenate([xa[:, :, 2 * L:]] * 2 + [xr[:, :, 2 * L:]] * 2, axis=1), 0.0)
    m_ab = xb[:, :2 * L]
    m_r = jnp.concatenate([xb[:, 2 * L:], xk[:, 2 * L:]], axis=2).astype(BF16)
    bk = jnp.concatenate([bt, kt], axis=1)
    v_st = split(v).astype(BF16)
    p_last = p_in[:, L - 1:L, :]
    bonus = head_sum(r * kx * r_k) * v
    gate = load(g_ref)

    n = _bdot(m_ab.astype(BF16), m_ab.astype(BF16))
    t_inv = eye + m_ab
    levels = int(math.log2(L))
    for level in range(1, levels):
        nb = n.astype(BF16)
        if level < levels - 1:
            prod = _bdot(nb, jnp.concatenate([nb, t_inv.astype(BF16)], axis=2))
            n, t_inv = prod[:, :, :2 * L], t_inv + prod[:, :, 2 * L:]
        else:
            t_inv = t_inv + _bdot(nb, t_inv.astype(BF16))
    t_inv = t_inv.astype(BF16)
    akv = _bdot(xk[:, :2 * L].astype(BF16), v_st)

    s_bd = state[...]
    for c in range(chunks):
        u = slice(c * lanes, (c + 1) * lanes)
        xs = _bdot_nt(lhs[u], s_bd.astype(BF16))
        sa_st = _bdot(t_inv[u], (split(xs[:, :L]) + akv[u]).astype(BF16))
        y_st = _bdot(m_r[u], jnp.concatenate([sa_st.astype(BF16), v_st[u]], axis=1))
        y = xs[:, L:] + y_st[:, :L] + y_st[:, L:]
        sa = sa_st[:, :L] + sa_st[:, L:]
        ds = _bdot_tn(jnp.concatenate([sa, v[u]], axis=1).astype(BF16), bk[u])
        s_bd = jnp.where(block_diag, s_bd + ds, 0.0) * p_last[u]

        mean = head_sum(y) * (1.0 / HEAD)
        dev = y - mean
        var = head_sum(dev * dev) * (1.0 / HEAD)
        yn = dev * lax.rsqrt(var + RWKV_GN_EPS) * lnx_w + lnx_b
        out = ((yn + bonus[u]) * gate[u]).astype(BF16)
        for q in range(seqs):
            for p in range(pairs):
                o_ref[q, c * L:(c + 1) * L, p * GROUP:(p + 1) * GROUP] = out[q * pairs + p]
    state[...] = s_bd


def _rwkv_rec(r, ld, k, v, kk, a, g, r_k, lnx_w, lnx_b, batch, seq, pairs, chunks, seqs):
    rows, d = r.shape
    width = pairs * GROUP
    groups = d // width
    nc = seq // (chunks * RWKV_L)
    spec = pl.BlockSpec((seqs, chunks * RWKV_L, width), lambda b, p, c: (b, c, p))
    vec = pl.BlockSpec((1, width), lambda b, p, c: (0, p))
    per_seq = [t.reshape(batch, seq, d) for t in (r, ld, k, v, kk, a, g)]
    out = pl.pallas_call(
        _rwkv_rec_kernel,
        grid=(batch // seqs, groups, nc),
        in_specs=[spec] * 7 + [vec] * 3,
        out_specs=spec,
        out_shape=jax.ShapeDtypeStruct((batch, seq, d), BF16),
        scratch_shapes=[pltpu.VMEM((seqs * pairs, GROUP, GROUP), F32)],
        compiler_params=_params(("parallel", "parallel", "arbitrary")),
        name="rwkv_recurrence",
    )(*per_seq, r_k, lnx_w, lnx_b)
    return out.reshape(rows, d)


def _rotary_tables(seq):
    inv_freq = 1.0 / (ROPE_THETA ** (jnp.arange(0, HEAD, 2, dtype=F32) / HEAD))
    ang = jnp.arange(seq, dtype=F32)[:, None] * inv_freq[None, :]
    cos, sin = jnp.cos(ang), jnp.sin(ang)
    cos = jnp.concatenate([cos, cos, cos, cos], axis=1)
    sin = jnp.concatenate([-sin, sin, -sin, sin], axis=1)
    return cos, sin


def _pad_cols(w):
    return jnp.pad(w, ((0, 0), (0, LORA_PAD - w.shape[1])))


def _pad_rows(w):
    return jnp.pad(w, ((0, LORA_PAD - w.shape[0]), (0, 0)))


def kernel(x, e_norm_g, e_w_in, e_pool_w, e_pool_scale, e_lambda, e_subln_g, e_w_o, o_norm_g, o_mu, o_w_r, o_w_k, o_w_v, o_w_o, o_w0, o_w1, o_w2, o_a0, o_a1, o_a2, o_g1, o_g2, o_k_k, o_k_a, o_r_k, o_lnx_w, o_lnx_b, f_norm_g, f_w_in, f_conv_w, f_conv_b, f_w_out, final_g):
    batch, seq, d = x.shape
    depth = f_norm_g.shape[0]
    tm = min(512, seq)
    attn_t = min(512, seq)
    x2 = x.reshape(batch * seq, d)
    row = lambda vec: vec.reshape(1, -1)
    cos, sin = _rotary_tables(seq)

    f_w_in_b, f_w_out_b = f_w_in.astype(BF16), f_w_out.astype(BF16)

    for layer in range(depth):
        i = layer // 2
        if layer % 2 == 0:
            lambda_init = 0.8 - 0.6 * math.exp(-0.3 * layer)
            pool_out, q, k, v = _even_proj(x2, row(e_norm_g[i]), e_w_in[i].astype(BF16), cos, sin,
                                           e_pool_w[i].astype(BF16), row(e_pool_scale[i]), seq, tm)
            attn_out = _diff_attention(q, k, v, e_lambda[i], row(e_subln_g[i]), batch, seq,
                                       lambda_init, attn_t)
            acts, w_mix = [pool_out, attn_out], e_w_o[i].astype(BF16)
        else:
            r, ld, k, v, kk, a, g = _rwkv_proj(
                x2, row(o_norm_g[i]), o_mu[i], o_w_r[i].astype(BF16), o_w_k[i].astype(BF16),
                o_w_v[i].astype(BF16), row(o_w0[i]), _pad_cols(o_w1[i]).astype(BF16),
                _pad_rows(o_w2[i]).astype(BF16), row(o_a0[i]), _pad_cols(o_a1[i]).astype(BF16),
                _pad_rows(o_a2[i]).astype(BF16), o_g1[i].astype(BF16), o_g2[i].astype(BF16),
                row(o_k_k[i]), row(o_k_a[i]), seq, tm)
            z = _rwkv_rec(r, ld, k, v, kk, a, g, row(o_r_k[i]), row(o_lnx_w[i]), row(o_lnx_b[i]),
                          batch, seq, pairs=8, chunks=4, seqs=2 if batch % 2 == 0 else 1)
            acts, w_mix = [z], o_w_o[i].astype(BF16)
        x2 = _ffn(x2, acts, w_mix, row(f_norm_g[layer]), f_w_in_b, f_conv_w[layer],
                  row(f_conv_b[layer]), f_w_out_b, row(final_g), layer, seq,
                  min(1024, seq), tf=256, final_norm=(layer == depth - 1))
    return x2.reshape(batch, seq, d)
```
